```python
import math
import jax, jax.numpy as jnp
from jax import lax
import numpy as np

D_MODEL = 1024
BATCH = 2
SEQ = 8192
DEPTH = 2

EPS = 1e-6
Q_BLOCK = 128
SB_HEADS = 4
SB_HEAD_DIM = 64
SB_WIDTH = SB_HEADS * SB_HEAD_DIM
SSM_HEADS = 8
SSM_HEAD_DIM = 64
SSM_INNER = SSM_HEADS * SSM_HEAD_DIM
SSM_GROUPS = 2
SSM_STATE = 64
SSM_CONV = 4
SSM_CHUNK = 128
SSM_CONV_DIM = SSM_INNER + 2 * SSM_GROUPS * SSM_STATE
MLA_HEADS = 4
MLA_NOPE = 64
MLA_ROPE = 32
MLA_V = 64
MLA_Q_RANK = 256
MLA_KV_RANK = 128
MLA_WIDTH = MLA_HEADS * MLA_V
ROPE_THETA = 10000.0
D_MIX = SB_WIDTH + SSM_INNER + MLA_WIDTH
IN_SPLITS = (3 * SB_WIDTH, SSM_INNER, SSM_CONV_DIM, SSM_HEADS, MLA_Q_RANK, MLA_KV_RANK, MLA_ROPE)
D_IN = 3 * SB_WIDTH + SSM_INNER + SSM_CONV_DIM + SSM_HEADS + MLA_Q_RANK + MLA_KV_RANK + MLA_ROPE
D_FF = 2816
FFN_CONV = 3

kernel_name = "hymba_sb_ssd_mla_convffn"


def rms_norm(x, g):
    xf = x.astype(jnp.float32)
    y = xf * lax.rsqrt(jnp.mean(xf * xf, axis=-1, keepdims=True) + EPS)
    return (y * g.astype(jnp.float32)).astype(x.dtype)


def causal_depthwise_conv(x, w, b):
    k = w.shape[0]
    c = x.shape[-1]
    y = lax.conv_general_dilated(
        x, w[:, None, :].astype(x.dtype), window_strides=(1,), padding=((k - 1, 0),),
        dimension_numbers=('NWC', 'WIO', 'NWC'), feature_group_count=c)
    return y + b.astype(y.dtype)


def query_blocks(fn, q):
    b, s, h, d = q.shape
    nb = s // Q_BLOCK
    qb = q.reshape(b, nb, Q_BLOCK, h, d).transpose(1, 0, 2, 3, 4)
    starts = jnp.arange(nb, dtype=jnp.int32) * Q_BLOCK
    out = lax.map(lambda a: fn(a[0], a[1]), (qb, starts))
    return out.transpose(1, 0, 2, 3, 4).reshape(b, s, h, out.shape[-1])


def stick_breaking_attention(q, k, v):
    s = k.shape[1]
    scale = SB_HEAD_DIM ** -0.5
    k_idx = jnp.arange(s)

    def block(qb, t0):
        z = jnp.einsum('bqhd,bkhd->bhqk', qb, k).astype(jnp.float32) * scale
        q_idx = t0 + jnp.arange(Q_BLOCK)
        mask = k_idx[None, :] < q_idx[:, None]
        log_keep = jnp.where(mask, jax.nn.log_sigmoid(-z), 0.0)
        later = lax.cumsum(log_keep, axis=3, reverse=True) - log_keep
        log_w = jnp.where(mask, jax.nn.log_sigmoid(z) + later, -jnp.inf)
        w = jnp.exp(log_w)
        return jnp.einsum('bhqk,bkhd->bqhd', w.astype(v.dtype), v)

    return query_blocks(block, q)


def causal_softmax_attention(q, k, v, scale):
    s = k.shape[1]
    k_idx = jnp.arange(s)

    def block(qb, t0):
        sc = jnp.einsum('bqhd,bkhd->bhqk', qb, k).astype(jnp.float32) * scale
        q_idx = t0 + jnp.arange(Q_BLOCK)
        mask = k_idx[None, :] <= q_idx[:, None]
        p = jax.nn.softmax(jnp.where(mask, sc, -jnp.inf), axis=-1)
        return jnp.einsum('bhqk,bkhd->bqhd', p.astype(v.dtype), v)

    return query_blocks(block, q)


def ssd_scan(xs, dt, a, bm, cm):
    f32 = jnp.float32
    b, s, h, p = xs.shape
    g, n = bm.shape[-2:]
    r = h // g
    l = SSM_CHUNK
    nc = s // l
    x = xs.astype(f32).reshape(b, nc, l, g, r, p)
    dt = dt.astype(f32).reshape(b, nc, l, g, r)
    bm = bm.astype(f32).reshape(b, nc, l, g, n)
    cm = cm.astype(f32).reshape(b, nc, l, g, n)
    da_cs = jnp.cumsum(dt * a.astype(f32).reshape(g, r), axis=2)
    xdt = x * dt[..., None]
    causal = jnp.tril(jnp.ones((l, l), dtype=bool))
    seg = da_cs[:, :, :, None] - da_cs[:, :, None, :]
    decay = jnp.exp(jnp.where(causal[:, :, None, None], seg, -jnp.inf))
    cb = jnp.einsum('bclgn,bcsgn->bclsg', cm, bm)
    y_diag = jnp.einsum('bclsg,bclsgr,bcsgrp->bclgrp', cb, decay, xdt)
    to_end = jnp.exp(da_cs[:, :, -1:] - da_cs)
    states = jnp.einsum('bclgn,bclgr,bclgrp->bcgrpn', bm, to_end, xdt)
    chunk_decay = jnp.exp(da_cs[:, :, -1])

    def step(h_state, inp):
        st, dc = inp
        return h_state * dc[..., None, None] + st, h_state

    h0 = jnp.zeros((b, g, r, p, n), f32)
    _, prev = lax.scan(step, h0, (states.transpose(1, 0, 2, 3, 4, 5), chunk_decay.transpose(1, 0, 2, 3)))
    prev = prev.transpose(1, 0, 2, 3, 4, 5)
    y_off = jnp.einsum('bclgn,bcgrpn,bclgr->bclgrp', cm, prev, jnp.exp(da_cs))
    return (y_diag + y_off).reshape(b, s, h, p)


def rope_tables(positions):
    inv_freq = 1.0 / (ROPE_THETA ** (jnp.arange(0, MLA_ROPE, 2, dtype=jnp.float32) / MLA_ROPE))
    ang = positions.astype(jnp.float32)[..., None] * inv_freq
    return jnp.cos(ang)[:, :, None, :], jnp.sin(ang)[:, :, None, :]


def apply_rope(x, cos, sin):
    xf = x.astype(jnp.float32)
    x1, x2 = xf[..., :MLA_ROPE // 2], xf[..., MLA_ROPE // 2:]
    return jnp.concatenate([x1 * cos - x2 * sin, x2 * cos + x1 * sin], axis=-1).astype(x.dtype)


def hybrid_mixer(h, positions, w_in, sb_out_norm, conv_w, conv_b, dt_bias, a_log, d_skip,
                 ssm_out_norm, q_norm, w_uq, kv_norm, w_ukv, mla_out_norm, w_out):
    b, s, _ = h.shape
    proj = h @ w_in
    offs = [int(o) for o in np.cumsum(IN_SPLITS)[:-1]]
    sb_qkv, z, xbc, dt_raw, c_q, c_kv, k_rope = jnp.split(proj, offs, axis=-1)

    qkv = sb_qkv.reshape(b, s, 3, SB_HEADS, SB_HEAD_DIM)
    y_a = stick_breaking_attention(qkv[:, :, 0], qkv[:, :, 1], qkv[:, :, 2]).reshape(b, s, SB_WIDTH)
    y_a = rms_norm(y_a, sb_out_norm)

    xbc = jax.nn.silu(causal_depthwise_conv(xbc, conv_w, conv_b))
    gn = SSM_GROUPS * SSM_STATE
    x_ssm, b_ssm, c_ssm = jnp.split(xbc, [SSM_INNER, SSM_INNER + gn], axis=-1)
    dt = jax.nn.softplus(dt_raw.astype(jnp.float32) + dt_bias.astype(jnp.float32))
    a = -jnp.exp(a_log.astype(jnp.float32))
    x_heads = x_ssm.reshape(b, s, SSM_HEADS, SSM_HEAD_DIM)
    y_ssm = ssd_scan(x_heads, dt, a,
                     b_ssm.reshape(b, s, SSM_GROUPS, SSM_STATE),
                     c_ssm.reshape(b, s, SSM_GROUPS, SSM_STATE))
    y_ssm = y_ssm + x_heads.astype(jnp.float32) * d_skip.astype(jnp.float32)[:, None]
    y_b = rms_norm(y_ssm.reshape(b, s, SSM_INNER) * jax.nn.silu(z.astype(jnp.float32)),
                   ssm_out_norm).astype(h.dtype)

    cos, sin = rope_tables(positions)
    q_c = (rms_norm(c_q, q_norm) @ w_uq).reshape(b, s, MLA_HEADS, MLA_NOPE + MLA_ROPE)
    q_full = jnp.concatenate([q_c[..., :MLA_NOPE], apply_rope(q_c[..., MLA_NOPE:], cos, sin)], axis=-1)
    kv = (rms_norm(c_kv, kv_norm) @ w_ukv).reshape(b, s, MLA_HEADS, MLA_NOPE + MLA_V)
    k_pe = apply_rope(k_rope[:, :, None, :], cos, sin)
    k_full = jnp.concatenate(
        [kv[..., :MLA_NOPE], jnp.broadcast_to(k_pe, (b, s, MLA_HEADS, MLA_ROPE))], axis=-1)
    y_c = causal_softmax_attention(q_full, k_full, kv[..., MLA_NOPE:],
                                   (MLA_NOPE + MLA_ROPE) ** -0.5).reshape(b, s, MLA_WIDTH)
    y_c = rms_norm(y_c, mla_out_norm)

    return jnp.concatenate([y_a, y_b, y_c], axis=-1) @ w_out


def conv_glu_ffn(h, w_up, conv_w, conv_b, w_down):
    u = causal_depthwise_conv(h @ w_up, conv_w, conv_b)
    gate, val = jnp.split(u, 2, axis=-1)
    return (jax.nn.silu(gate) * val) @ w_down


def setup_inputs(seed: int = 0) -> dict:
    key = jax.random.key(seed)
    ks = jax.random.split(key, 24)
    L = DEPTH
    nrm = jax.random.normal

    def gain(k, shape):
        return 1.0 + 0.02 * nrm(k, shape, jnp.float32)

    x = nrm(ks[0], (BATCH, SEQ, D_MODEL), jnp.float32)
    offset = jax.random.randint(ks[1], (BATCH, 1), 0, 4096)
    positions = (jnp.arange(SEQ, dtype=jnp.int32)[None, :] + offset).astype(jnp.int32)
    dt0 = jnp.exp(jax.random.uniform(ks[8], (L, SSM_HEADS), jnp.float32)
                  * (math.log(0.1) - math.log(0.001)) + math.log(0.001))
    return {
        "x": x,
        "positions": positions,
        "mix_norm": gain(ks[2], (L, D_MODEL)),
        "w_in": nrm(ks[3], (L, D_MODEL, D_IN), jnp.float32) * D_MODEL ** -0.5,
        "sb_out_norm": gain(ks[4], (L, SB_WIDTH)),
        "ssm_conv_w": nrm(ks[5], (L, SSM_CONV, SSM_CONV_DIM), jnp.float32) * SSM_CONV ** -0.5,
        "ssm_conv_b": 0.01 * nrm(ks[6], (L, SSM_CONV_DIM), jnp.float32),
        "ssm_dt_bias": dt0 + jnp.log(-jnp.expm1(-dt0)),
        "ssm_a_log": jnp.log(jax.random.uniform(ks[9], (L, SSM_HEADS), jnp.float32, 1.0, 16.0)),
        "ssm_d": 1.0 + 0.1 * nrm(ks[10], (L, SSM_HEADS), jnp.float32),
        "ssm_out_norm": gain(ks[11], (L, SSM_INNER)),
        "mla_q_norm": gain(ks[12], (L, MLA_Q_RANK)),
        "mla_w_uq": nrm(ks[13], (L, MLA_Q_RANK, MLA_HEADS * (MLA_NOPE + MLA_ROPE)), jnp.float32) * MLA_Q_RANK ** -0.5,
        "mla_kv_norm": gain(ks[14], (L, MLA_KV_RANK)),
        "mla_w_ukv": nrm(ks[15], (L, MLA_KV_RANK, MLA_HEADS * (MLA_NOPE + MLA_V)), jnp.float32) * MLA_KV_RANK ** -0.5,
        "mla_out_norm": gain(ks[16], (L, MLA_WIDTH)),
        "w_out": nrm(ks[17], (L, D_MIX, D_MODEL), jnp.float32) * D_MIX ** -0.5,
        "ffn_norm": gain(ks[18], (L, D_MODEL)),
        "ffn_w_up": nrm(ks[19], (L, D_MODEL, 2 * D_FF), jnp.float32) * D_MODEL ** -0.5,
        "ffn_conv_w": nrm(ks[20], (L, FFN_CONV, 2 * D_FF), jnp.float32) * FFN_CONV ** -0.5,
        "ffn_conv_b": 0.01 * nrm(ks[21], (L, 2 * D_FF), jnp.float32),
        "ffn_w_down": nrm(ks[22], (L, D_FF, D_MODEL), jnp.float32) * D_FF ** -0.5,
        "final_norm": gain(ks[23], (D_MODEL,)),
    }


def reference(x, positions, mix_norm, w_in, sb_out_norm, ssm_conv_w, ssm_conv_b, ssm_dt_bias,
              ssm_a_log, ssm_d, ssm_out_norm, mla_q_norm, mla_w_uq, mla_kv_norm, mla_w_ukv,
              mla_out_norm, w_out, ffn_norm, ffn_w_up, ffn_conv_w, ffn_conv_b, ffn_w_down,
              final_norm):
    h = x
    for i in range(DEPTH):
        h = h + hybrid_mixer(rms_norm(h, mix_norm[i]), positions, w_in[i], sb_out_norm[i],
                             ssm_conv_w[i], ssm_conv_b[i], ssm_dt_bias[i], ssm_a_log[i], ssm_d[i],
                             ssm_out_norm[i], mla_q_norm[i], mla_w_uq[i], mla_kv_norm[i],
                             mla_w_ukv[i], mla_out_norm[i], w_out[i])
        h = h + conv_glu_ffn(rms_norm(h, ffn_norm[i]), ffn_w_up[i], ffn_conv_w[i],
                             ffn_conv_b[i], ffn_w_down[i])
    return rms_norm(h, final_norm)
```

```python
import functools

import jax
import jax.numpy as jnp
from jax import lax
from jax.experimental import pallas as pl
from jax.experimental.pallas import tpu as pltpu

F32 = jnp.float32
BF16 = jnp.bfloat16

D_MODEL = 1024
EPS = 1e-6
SB_HEADS = 4
SB_HEAD_DIM = 64
SB_WIDTH = SB_HEADS * SB_HEAD_DIM
SSM_HEADS = 8
SSM_HEAD_DIM = 64
SSM_INNER = SSM_HEADS * SSM_HEAD_DIM
SSM_GROUPS = 2
SSM_STATE = 64
SSM_CONV = 4
SSM_CHUNK = 128
SSM_CONV_DIM = SSM_INNER + 2 * SSM_GROUPS * SSM_STATE
MLA_HEADS = 4
MLA_NOPE = 64
MLA_ROPE = 32
MLA_V = 64
MLA_Q_RANK = 256
MLA_KV_RANK = 128
MLA_WIDTH = MLA_HEADS * MLA_V
ROPE_THETA = 10000.0
D_MIX = SB_WIDTH + SSM_INNER + MLA_WIDTH
D_FF = 2816
FFN_CONV = 3

LANES = 128
SUBLANES = 8
MLA_PAD = 128
HALF = MLA_ROPE // 2
VMEM_LIMIT = 48 * 1024 * 1024

C_SB = 0
C_Z = C_SB + 3 * SB_WIDTH
C_XBC = C_Z + SSM_INNER
C_CQ = C_XBC + SSM_CONV_DIM
C_CKV = C_CQ + MLA_Q_RANK
C_KPE = C_CKV + MLA_KV_RANK
C_DT = C_KPE + MLA_PAD
N_IN = C_DT + LANES

SB_LOG_FLOOR = -104.0


def _rms(x, g):
    return x * lax.rsqrt(jnp.mean(x * x, axis=-1, keepdims=True) + EPS) * g


def _silu(x):
    return x * (1.0 / (1.0 + jnp.exp(-x)))


def _softplus(x):
    return jnp.maximum(x, 0.0) + jnp.log1p(jnp.exp(-jnp.abs(x)))


def _split3(x):
    a = x.astype(BF16)
    r = x - a.astype(F32)
    b = r.astype(BF16)
    c = (r - b.astype(F32)).astype(BF16)
    return a, b, c


def _dot(a, b):
    return jnp.dot(a, b, preferred_element_type=F32)


def _dot_nt(a, b):
    return lax.dot_general(a, b, (((1,), (1,)), ((), ())), preferred_element_type=F32)


def _dot_tn(a, b):
    return lax.dot_general(a, b, (((0,), (0,)), ((), ())), preferred_element_type=F32)


def _rope_table_kernel(pos_ref, invf_ref, sign_ref, cos_ref, sin_ref):
    ang = pos_ref[...].astype(F32) * invf_ref[...]
    cos_ref[...] = jnp.cos(ang)
    sin_ref[...] = jnp.sin(ang) * sign_ref[...]


def _rope_tables(pos_col, invf_lane, sign_lane, tm):
    t = pos_col.shape[0]
    return pl.pallas_call(
        _rope_table_kernel,
        grid=(t // tm,),
        in_specs=[pl.BlockSpec((tm, 1), lambda i: (i, 0)),
                  pl.BlockSpec((1, LANES), lambda i: (0, 0)),
                  pl.BlockSpec((1, LANES), lambda i: (0, 0))],
        out_specs=[pl.BlockSpec((tm, LANES), lambda i: (i, 0)),
                   pl.BlockSpec((tm, LANES), lambda i: (i, 0))],
        out_shape=[jax.ShapeDtypeStruct((t, LANES), F32)] * 2,
        compiler_params=pltpu.CompilerParams(dimension_semantics=("parallel",)),
        name="rope_tables",
    )(pos_col, invf_lane, sign_lane)


def _mix_in_kernel(h_ref, g_ref, w_ref, qn_ref, wuq_ref, kvn_ref, wk_ref, wv_ref, cos_ref, sin_ref,
                   sb_ref, z_ref, xbc_ref, dt_ref, qm_ref, km_ref, vm_ref):
    xn = _rms(h_ref[...], g_ref[...]).astype(BF16)

    def proj(lo, hi):
        return _dot(xn, w_ref[:, lo:hi])

    sb = proj(C_SB, C_Z)
    sb_ref[:, :SB_WIDTH] = (sb[:, :SB_WIDTH] * (SB_HEAD_DIM ** -0.5)).astype(BF16)
    sb_ref[:, SB_WIDTH:] = sb[:, SB_WIDTH:].astype(BF16)
    z_ref[...] = proj(C_Z, C_XBC)
    xbc_ref[...] = proj(C_XBC, C_CQ)
    dt_ref[...] = proj(C_DT, N_IN)

    cos = cos_ref[...]
    sin = sin_ref[...]

    def rope(x):
        return x * cos + pltpu.roll(x, LANES // 2, axis=1) * sin

    cq = _rms(proj(C_CQ, C_CKV), qn_ref[...]).astype(BF16)
    qf = _dot(cq, wuq_ref[...])
    ckv = _rms(proj(C_CKV, C_KPE), kvn_ref[...]).astype(BF16)
    kn = _dot(ckv, wk_ref[...])
    vm_ref[...] = _dot(ckv, wv_ref[...]).astype(BF16)
    kpe = rope(proj(C_KPE, C_DT))
    for h in range(MLA_HEADS):
        sl = slice(h * MLA_PAD, (h + 1) * MLA_PAD)
        qm_ref[:, sl] = rope(qf[:, sl]).astype(BF16)
        km_ref[:, sl] = (kn[:, sl] + kpe).astype(BF16)


def _mix_in(h, g, w_in_p, qn, wuq_p, kvn, wk_p, wv_p, cos_t, sin_t, tm):
    t = h.shape[0]
    row = lambda n: pl.BlockSpec((tm, n), lambda i: (i, 0))
    full = lambda a: pl.BlockSpec(a.shape, lambda i: (0, 0))
    out_widths = [(3 * SB_WIDTH, BF16), (SSM_INNER, F32), (SSM_CONV_DIM, F32), (LANES, F32),
                  (MLA_HEADS * MLA_PAD, BF16), (MLA_HEADS * MLA_PAD, BF16), (MLA_WIDTH, BF16)]
    return pl.pallas_call(
        _mix_in_kernel,
        grid=(t // tm,),
        in_specs=[row(D_MODEL), full(g), full(w_in_p), full(qn), full(wuq_p), full(kvn), full(wk_p),
                  full(wv_p), row(LANES), row(LANES)],
        out_specs=[row(n) for n, _ in out_widths],
        out_shape=[jax.ShapeDtypeStruct((t, n), d) for n, d in out_widths],
        compiler_params=pltpu.CompilerParams(dimension_semantics=("parallel",),
                                             vmem_limit_bytes=VMEM_LIMIT),
        name="mix_in",
    )(h, g, w_in_p, qn, wuq_p, kvn, wk_p, wv_p, cos_t, sin_t)


def _sb_kernel(q_ref, k_ref, v_ref, o_ref, *, tq):
    tk = tq
    qi = pl.program_id(1)
    row = lax.broadcasted_iota(jnp.int32, (tq, tk), 0)
    col = lax.broadcasted_iota(jnp.int32, (tq, tk), 1)
    upper = (lax.broadcasted_iota(jnp.int32, (tk, tk), 0) >
             lax.broadcasted_iota(jnp.int32, (tk, tk), 1)).astype(BF16)

    for h in range(SB_HEADS):
        hs = slice(h * SB_HEAD_DIM, (h + 1) * SB_HEAD_DIM)
        qh = q_ref[:, hs]

        def body(carry, hs=hs, qh=qh):
            j, c, acc = carry
            ks = pl.multiple_of(j * tk, tk)
            kb = k_ref[pl.ds(ks, tk), hs]
            vb = v_ref[pl.ds(ks, tk), hs]
            z = _dot_nt(qh, kb)
            mask = (col + (j - qi) * tk) < row
            log_keep = jnp.where(mask, -_softplus(z), 0.0)
            hi = log_keep.astype(BF16)
            lo = (log_keep - hi.astype(F32)).astype(BF16)
            later = _dot(hi, upper) + _dot(lo, upper) + c
            w = jnp.where(mask, jnp.exp(z + log_keep + later), 0.0)
            acc = acc + _dot(w.astype(BF16), vb)
            c = c + jnp.sum(log_keep, axis=1, keepdims=True)
            return j - 1, c, acc

        def cond(carry):
            j, c, _ = carry
            return jnp.logical_and(j >= 0, jnp.max(c) > SB_LOG_FLOOR)

        init = (qi, jnp.zeros((tq, 1), F32), jnp.zeros((tq, SB_HEAD_DIM), F32))
        _, _, acc = lax.while_loop(cond, body, init)
        o_ref[:, hs] = acc


def _sb_attn(sb, b, s, tq):
    nq = s // tq
    return pl.pallas_call(
        functools.partial(_sb_kernel, tq=tq),
        grid=(b, nq),
        in_specs=[pl.BlockSpec((tq, SB_WIDTH), lambda bi, qi: (bi * nq + qi, 0)),
                  pl.BlockSpec((s, SB_WIDTH), lambda bi, qi: (bi, 1)),
                  pl.BlockSpec((s, SB_WIDTH), lambda bi, qi: (bi, 2))],
        out_specs=pl.BlockSpec((tq, SB_WIDTH), lambda bi, qi: (bi * nq + qi, 0)),
        out_shape=jax.ShapeDtypeStruct((b * s, SB_WIDTH), F32),
        compiler_params=pltpu.CompilerParams(dimension_semantics=("parallel", "parallel"),
                                             vmem_limit_bytes=VMEM_LIMIT),
        name="sb_attn",
    )(sb, sb, sb)


def _ssd_kernel(xbc_ref, z_ref, dt_ref, cw_ref, cb_ref, dtb_ref, alog_ref, dl_ref, gn_ref,
                y_ref, tail_ref, xs_ref, state_ref):
    l = SSM_CHUNK

    @pl.when(pl.program_id(1) == 0)
    def _():
        tail_ref[...] = jnp.zeros_like(tail_ref)
        state_ref[...] = jnp.zeros_like(state_ref)

    xs_ref[0:SUBLANES, :] = tail_ref[...]
    xs_ref[SUBLANES:SUBLANES + l, :] = xbc_ref[...]
    tail_ref[...] = xbc_ref[l - SUBLANES:l, :]
    conv = cb_ref[...]
    for k in range(SSM_CONV):
        conv = conv + cw_ref[k:k + 1, :] * xs_ref[pl.ds(SUBLANES - (SSM_CONV - 1) + k, l), :]
    xa = _silu(conv)
    x = xa[:, :SSM_INNER]
    gn = SSM_GROUPS * SSM_STATE
    bm = xa[:, SSM_INNER:SSM_INNER + gn]
    cm = xa[:, SSM_INNER + gn:]

    dt = _softplus(dt_ref[...] + dtb_ref[...])
    da = dt * (-jnp.exp(alog_ref[...]))
    tril = (lax.broadcasted_iota(jnp.int32, (l, l), 0) >=
            lax.broadcasted_iota(jnp.int32, (l, l), 1))
    trilb = tril.astype(BF16)
    d1, d2, d3 = _split3(da)
    cs = _dot(trilb, d1) + _dot(trilb, d2) + _dot(trilb, d3)
    cs_t = cs.T
    cs_last = cs[l - 1:l, :]

    ys = []
    for g in range(SSM_GROUPS):
        bg = bm[:, g * SSM_STATE:(g + 1) * SSM_STATE]
        cg = cm[:, g * SSM_STATE:(g + 1) * SSM_STATE].astype(BF16)
        cb = _dot_nt(cg, bg.astype(BF16))
        for r in range(SSM_HEADS // SSM_GROUPS):
            h = g * (SSM_HEADS // SSM_GROUPS) + r
            hs = slice(h * SSM_HEAD_DIM, (h + 1) * SSM_HEAD_DIM)
            cs_col = cs[:, h:h + 1]
            decay = jnp.exp(jnp.where(tril, cs_col - cs_t[h:h + 1, :], -jnp.inf))
            xh = x[:, hs]
            xdt = (xh * dt[:, h:h + 1]).astype(BF16)
            y_diag = _dot((cb * decay).astype(BF16), xdt)
            st = state_ref[h]
            y_off = _dot(cg, st.astype(BF16)) * jnp.exp(cs_col)
            end = cs_last[:, h:h + 1]
            to_end = jnp.exp(end - cs_col)
            state_ref[h] = st * jnp.exp(end) + _dot_tn((bg * to_end).astype(BF16), xdt)
            ys.append(y_diag + y_off + xh * dl_ref[:, hs])
    y = jnp.concatenate(ys, axis=1) * _silu(z_ref[...])
    y_ref[...] = _rms(y, gn_ref[...]).astype(BF16)


def _ssd(xbc, z, dt, cw, cb, dtb, alog, dl, gn, b, s):
    l = SSM_CHUNK
    nc = s // l
    row = lambda n: pl.BlockSpec((l, n), lambda bi, ci: (bi * nc + ci, 0))
    full = lambda a: pl.BlockSpec(a.shape, lambda bi, ci: (0, 0))
    return pl.pallas_call(
        _ssd_kernel,
        grid=(b, nc),
        in_specs=[row(SSM_CONV_DIM), row(SSM_INNER), row(LANES), full(cw), full(cb), full(dtb),
                  full(alog), full(dl), full(gn)],
        out_specs=row(SSM_INNER),
        out_shape=jax.ShapeDtypeStruct((b * s, SSM_INNER), BF16),
        scratch_shapes=[pltpu.VMEM((SUBLANES, SSM_CONV_DIM), F32),
                        pltpu.VMEM((SUBLANES + l, SSM_CONV_DIM), F32),
                        pltpu.VMEM((SSM_HEADS, SSM_STATE, SSM_HEAD_DIM), F32)],
        compiler_params=pltpu.CompilerParams(dimension_semantics=("parallel", "arbitrary"),
                                             vmem_limit_bytes=VMEM_LIMIT),
        name="ssd",
    )(xbc, z, dt, cw, cb, dtb, alog, dl, gn)


def _mla_kernel(q_ref, k_ref, v_ref, o_ref, *, tq, scale):
    tk = tq
    qi = pl.program_id(2)
    row = lax.broadcasted_iota(jnp.int32, (tq, tk), 0)
    col = lax.broadcasted_iota(jnp.int32, (tq, tk), 1)
    outs = []
    for hh in range(2):
        hs = slice(hh * MLA_PAD, (hh + 1) * MLA_PAD)
        qh = q_ref[:, hs]

        def step(j, carry, masked, hs=hs, qh=qh):
            m, lsum, acc = carry
            ks = pl.multiple_of(j * tk, tk)
            sc = _dot_nt(qh, k_ref[pl.ds(ks, tk), hs]) * scale
            if masked:
                sc = jnp.where(col <= row, sc, -jnp.inf)
            m_new = jnp.maximum(m, jnp.max(sc, axis=1, keepdims=True))
            alpha = jnp.exp(m - m_new)
            p = jnp.exp(sc - m_new)
            lsum = alpha * lsum + jnp.sum(p, axis=1, keepdims=True)
            acc = alpha * acc + _dot(p.astype(BF16), v_ref[pl.ds(ks, tk), :])
            return m_new, lsum, acc

        init = (jnp.full((tq, 1), -1e30, F32), jnp.zeros((tq, 1), F32), jnp.zeros((tq, LANES), F32))
        carry = lax.fori_loop(0, qi, functools.partial(step, masked=False), init)
        _, lsum, acc = step(qi, carry, True)
        outs.append(acc / lsum)
    lane = lax.broadcasted_iota(jnp.int32, (tq, LANES), 1)
    o_ref[...] = jnp.where(lane < MLA_V, outs[0], outs[1])


def _mla_attn(qm, km, vm, b, s, tq):
    nq = s // tq
    scale = (MLA_NOPE + MLA_ROPE) ** -0.5
    return pl.pallas_call(
        functools.partial(_mla_kernel, tq=tq, scale=scale),
        grid=(b, MLA_HEADS // 2, nq),
        in_specs=[pl.BlockSpec((tq, 2 * MLA_PAD), lambda bi, hp, qi: (bi * nq + qi, hp)),
                  pl.BlockSpec((s, 2 * MLA_PAD), lambda bi, hp, qi: (bi, hp)),
                  pl.BlockSpec((s, 2 * MLA_V), lambda bi, hp, qi: (bi, hp))],
        out_specs=pl.BlockSpec((tq, 2 * MLA_V), lambda bi, hp, qi: (bi * nq + qi, hp)),
        out_shape=jax.ShapeDtypeStruct((b * s, MLA_WIDTH), F32),
        compiler_params=pltpu.CompilerParams(
            dimension_semantics=("parallel", "parallel", "parallel"), vmem_limit_bytes=VMEM_LIMIT),
        name="mla_attn",
    )(qm, km, vm)


def _mix_out_kernel(h_ref, ya_ref, yb_ref, yc_ref, ga_ref, gc_ref, w_ref, o_ref):
    ya = _rms(ya_ref[...], ga_ref[...]).astype(BF16)
    yc = _rms(yc_ref[...], gc_ref[...]).astype(BF16)
    c1 = SB_WIDTH
    c2 = SB_WIDTH + SSM_INNER
    o_ref[...] = (h_ref[...] + _dot(ya, w_ref[:c1, :]) + _dot(yb_ref[...], w_ref[c1:c2, :])
                  + _dot(yc, w_ref[c2:, :]))


def _mix_out(h, ya, yb, yc, ga, gc, w_out, tm):
    t = h.shape[0]
    row = lambda n: pl.BlockSpec((tm, n), lambda i: (i, 0))
    full = lambda a: pl.BlockSpec(a.shape, lambda i: (0, 0))
    return pl.pallas_call(
        _mix_out_kernel,
        grid=(t // tm,),
        in_specs=[row(D_MODEL), row(SB_WIDTH), row(SSM_INNER), row(MLA_WIDTH), full(ga), full(gc),
                  full(w_out)],
        out_specs=row(D_MODEL),
        out_shape=jax.ShapeDtypeStruct((t, D_MODEL), F32),
        compiler_params=pltpu.CompilerParams(dimension_semantics=("parallel",),
                                             vmem_limit_bytes=VMEM_LIMIT),
        name="mix_out",
    )(h, ya, yb, yc, ga, gc, w_out)


def _ffn_kernel(h_ref, g_ref, wg_ref, wv_ref, cwg_ref, cwv_ref, cbg_ref, cbv_ref, wd_ref, fin_ref,
                o_ref, hn_ref, acc_ref, carry_g, carry_v, sg_ref, sv_ref, *, tm, tiles_per_seq,
                final_norm):
    i = pl.program_id(0)
    j = pl.program_id(1)

    @pl.when(j == 0)
    def _():
        x = h_ref[...]
        hn_ref[...] = _rms(x, g_ref[...]).astype(BF16)
        acc_ref[...] = x

    first = (i % tiles_per_seq) == 0
    pad = SUBLANES

    def branch(w_ref, cw_ref, cb_ref, carry, s_ref):
        up = _dot(hn_ref[...], w_ref[...])
        s_ref[0:pad, :] = jnp.where(first, 0.0, carry[j])
        s_ref[pad:pad + tm, :] = up
        carry[j] = up[tm - pad:tm, :]
        return (cb_ref[...] + cw_ref[0:1, :] * s_ref[pl.ds(pad - 2, tm), :]
                + cw_ref[1:2, :] * s_ref[pl.ds(pad - 1, tm), :] + cw_ref[2:3, :] * up)

    gate = branch(wg_ref, cwg_ref, cbg_ref, carry_g, sg_ref)
    val = branch(wv_ref, cwv_ref, cbv_ref, carry_v, sv_ref)
    act = (_silu(gate) * val).astype(BF16)
    acc_ref[...] += _dot(act, wd_ref[...])

    @pl.when(j == pl.num_programs(1) - 1)
    def _():
        out = acc_ref[...]
        if final_norm:
            out = _rms(out, fin_ref[...])
        o_ref[...] = out


def _ffn(h, g, w_up, cw, cb, w_down, fin, s, tm, tf, final_norm):
    t = h.shape[0]
    nj = D_FF // tf
    full = lambda a: pl.BlockSpec(a.shape, lambda i, j: (0, 0))
    return pl.pallas_call(
        functools.partial(_ffn_kernel, tm=tm, tiles_per_seq=s // tm, final_norm=final_norm),
        grid=(t // tm, nj),
        in_specs=[pl.BlockSpec((tm, D_MODEL), lambda i, j: (i, 0)), full(g),
                  pl.BlockSpec((D_MODEL, tf), lambda i, j: (0, j)),
                  pl.BlockSpec((D_MODEL, tf), lambda i, j: (0, nj + j)),
                  pl.BlockSpec((FFN_CONV, tf), lambda i, j: (0, j)),
                  pl.BlockSpec((FFN_CONV, tf), lambda i, j: (0, nj + j)),
                  pl.BlockSpec((1, tf), lambda i, j: (0, j)),
                  pl.BlockSpec((1, tf), lambda i, j: (0, nj + j)),
                  pl.BlockSpec((tf, D_MODEL), lambda i, j: (j, 0)), full(fin)],
        out_specs=pl.BlockSpec((tm, D_MODEL), lambda i, j: (i, 0)),
        out_shape=jax.ShapeDtypeStruct((t, D_MODEL), F32),
        scratch_shapes=[pltpu.VMEM((tm, D_MODEL), BF16), pltpu.VMEM((tm, D_MODEL), F32),
                        pltpu.VMEM((nj, SUBLANES, tf), F32), pltpu.VMEM((nj, SUBLANES, tf), F32),
                        pltpu.VMEM((SUBLANES + tm, tf), F32), pltpu.VMEM((SUBLANES + tm, tf), F32)],
        compiler_params=pltpu.CompilerParams(dimension_semantics=("arbitrary", "arbitrary"),
                                             vmem_limit_bytes=VMEM_LIMIT),
        name="ffn",
    )(h, g, w_up, w_up, cw, cw, cb, cb, w_down, fin)


def _pack_in_proj(w):
    d = w.shape[0]
    o_dt = C_CQ
    o_cq = o_dt + SSM_HEADS
    o_kr = o_cq + MLA_Q_RANK + MLA_KV_RANK
    kr = w[:, o_kr:o_kr + MLA_ROPE]
    zc = lambda n: jnp.zeros((d, n), w.dtype)
    kpe = jnp.concatenate([zc(32), kr[:, :HALF], zc(16), zc(32), kr[:, HALF:], zc(16)], axis=1)
    dtb = jnp.concatenate([w[:, o_dt:o_cq], zc(LANES - SSM_HEADS)], axis=1)
    return jnp.concatenate([w[:, :o_dt], w[:, o_cq:o_kr], kpe, dtb], axis=1).astype(BF16)


def _pack_uq(w):
    r = w.shape[0]
    zc = lambda n: jnp.zeros((r, n), w.dtype)
    cols = []
    for h in range(MLA_HEADS):
        o = h * (MLA_NOPE + MLA_ROPE)
        nope = w[:, o:o + MLA_NOPE]
        rp = w[:, o + MLA_NOPE:o + MLA_NOPE + MLA_ROPE]
        cols += [nope[:, :32], rp[:, :HALF], zc(16), nope[:, 32:], rp[:, HALF:], zc(16)]
    return jnp.concatenate(cols, axis=1).astype(BF16)


def _pack_ukv(w):
    r = w.shape[0]
    zc = lambda n: jnp.zeros((r, n), w.dtype)
    kc, vc = [], []
    for h in range(MLA_HEADS):
        o = h * (MLA_NOPE + MLA_V)
        nope = w[:, o:o + MLA_NOPE]
        kc += [nope[:, :32], zc(32), nope[:, 32:], zc(32)]
        vc.append(w[:, o + MLA_NOPE:o + MLA_NOPE + MLA_V])
    return jnp.concatenate(kc, axis=1).astype(BF16), jnp.concatenate(vc, axis=1).astype(BF16)


def _lane_pad(v):
    return jnp.concatenate([v, jnp.zeros((LANES - v.shape[0],), v.dtype)])[None, :]


TM_IN = 256
TM_OUT = 512
TQ_SB = 256
TQ_MLA = 256
TM_FFN = 512
TF_FFN = 1408


def kernel(x, positions, mix_norm, w_in, sb_out_norm, ssm_conv_w, ssm_conv_b, ssm_dt_bias, ssm_a_log, ssm_d, ssm_out_norm, mla_q_norm, mla_w_uq, mla_kv_norm, mla_w_ukv, mla_out_norm, w_out, ffn_norm, ffn_w_up, ffn_conv_w, ffn_conv_b, ffn_w_down, final_norm):
    b, s, d = x.shape
    t = b * s
    depth = w_in.shape[0]
    h = x.reshape(t, d)

    inv_freq = 1.0 / (ROPE_THETA ** (jnp.arange(0, MLA_ROPE, 2, dtype=F32) / MLA_ROPE))
    z16, z32 = jnp.zeros((16,), F32), jnp.zeros((32,), F32)
    invf_lane = jnp.concatenate([z32, inv_freq, z16, z32, inv_freq, z16])[None, :]
    sign_lane = jnp.concatenate([z32, -jnp.ones((16,), F32), z16, z32, jnp.ones((16,), F32), z16])[None, :]
    cos_t, sin_t = _rope_tables(positions.reshape(t, 1), invf_lane, sign_lane, TM_IN)

    for i in range(depth):
        wk_p, wv_p = _pack_ukv(mla_w_ukv[i])
        sb, z, xbc, dt, qm, km, vm = _mix_in(
            h, mix_norm[i][None, :], _pack_in_proj(w_in[i]), mla_q_norm[i][None, :],
            _pack_uq(mla_w_uq[i]), mla_kv_norm[i][None, :], wk_p, wv_p, cos_t, sin_t, TM_IN)
        ya = _sb_attn(sb, b, s, TQ_SB)
        yb = _ssd(xbc, z, dt, ssm_conv_w[i], ssm_conv_b[i][None, :], _lane_pad(ssm_dt_bias[i]),
                  _lane_pad(ssm_a_log[i]), jnp.repeat(ssm_d[i], SSM_HEAD_DIM)[None, :],
                  ssm_out_norm[i][None, :], b, s)
        yc = _mla_attn(qm, km, vm, b, s, TQ_MLA)
        h = _mix_out(h, ya, yb, yc, sb_out_norm[i][None, :], mla_out_norm[i][None, :],
                     w_out[i].astype(BF16), TM_OUT)
        h = _ffn(h, ffn_norm[i][None, :], ffn_w_up[i].astype(BF16), ffn_conv_w[i],
                 ffn_conv_b[i][None, :], ffn_w_down[i].astype(BF16), final_norm[None, :], s,
                 TM_FFN, TF_FFN, final_norm=(i == depth - 1))
    return h.reshape(b, s, d)
```

```python
import functools

import jax
import jax.numpy as jnp
from jax import lax
from jax.experimental import pallas as pl
from jax.experimental.pallas import tpu as pltpu

F32 = jnp.float32
BF16 = jnp.bfloat16

D_MODEL = 1024
EPS = 1e-6
SB_HEADS = 4
SB_HEAD_DIM = 64
SB_WIDTH = SB_HEADS * SB_HEAD_DIM
SSM_HEADS = 8
SSM_HEAD_DIM = 64
SSM_INNER = SSM_HEADS * SSM_HEAD_DIM
SSM_GROUPS = 2
SSM_STATE = 64
SSM_CONV = 4
SSM_CHUNK = 128
SSM_CONV_DIM = SSM_INNER + 2 * SSM_GROUPS * SSM_STATE
MLA_HEADS = 4
MLA_NOPE = 64
MLA_ROPE = 32
MLA_V = 64
MLA_Q_RANK = 256
MLA_KV_RANK = 128
MLA_WIDTH = MLA_HEADS * MLA_V
ROPE_THETA = 10000.0
D_MIX = SB_WIDTH + SSM_INNER + MLA_WIDTH
D_FF = 2816
FFN_CONV = 3

LANES = 128
SUBLANES = 8
MLA_PAD = 128
HALF = MLA_ROPE // 2
VMEM_LIMIT = 48 * 1024 * 1024

C_SB = 0
C_Z = C_SB + 3 * SB_WIDTH
C_XBC = C_Z + SSM_INNER
C_CQ = C_XBC + SSM_CONV_DIM
C_CKV = C_CQ + MLA_Q_RANK
C_KPE = C_CKV + MLA_KV_RANK
C_DT = C_KPE + MLA_PAD
N_IN = C_DT + LANES

MLA_Q_SCALE = (MLA_NOPE + MLA_ROPE) ** -0.5 * 1.4426950408889634

SB_LOG_FLOOR = -104.0


def _rms(x, g):
    return x * lax.rsqrt(jnp.mean(x * x, axis=-1, keepdims=True) + EPS) * g


def _silu(x):
    return x * (1.0 / (1.0 + jnp.exp(-x)))


def _softplus(x):
    return jnp.maximum(x, 0.0) + jnp.log1p(jnp.exp(-jnp.abs(x)))


def _split3(x):
    a = x.astype(BF16)
    r = x - a.astype(F32)
    b = r.astype(BF16)
    c = (r - b.astype(F32)).astype(BF16)
    return a, b, c


def _dot(a, b):
    return jnp.dot(a, b, preferred_element_type=F32)


def _dot_nt(a, b):
    return lax.dot_general(a, b, (((1,), (1,)), ((), ())), preferred_element_type=F32)


def _dot_tn(a, b):
    return lax.dot_general(a, b, (((0,), (0,)), ((), ())), preferred_element_type=F32)


def _rope_table_kernel(pos_ref, invf_ref, sign_ref, cos_ref, sin_ref):
    ang = pos_ref[...].astype(F32) * invf_ref[...]
    cos_ref[...] = jnp.cos(ang)
    sin_ref[...] = jnp.sin(ang) * sign_ref[...]


def _rope_tables(pos_col, invf_lane, sign_lane, tm):
    t = pos_col.shape[0]
    return pl.pallas_call(
        _rope_table_kernel,
        grid=(t // tm,),
        in_specs=[pl.BlockSpec((tm, 1), lambda i: (i, 0)),
                  pl.BlockSpec((1, LANES), lambda i: (0, 0)),
                  pl.BlockSpec((1, LANES), lambda i: (0, 0))],
        out_specs=[pl.BlockSpec((tm, LANES), lambda i: (i, 0)),
                   pl.BlockSpec((tm, LANES), lambda i: (i, 0))],
        out_shape=[jax.ShapeDtypeStruct((t, LANES), F32)] * 2,
        compiler_params=pltpu.CompilerParams(dimension_semantics=("parallel",)),
        name="rope_tables",
    )(pos_col, invf_lane, sign_lane)


def _mix_in_kernel(h_ref, g_ref, w_ref, qn_ref, wuq_ref, kvn_ref, wk_ref, wv_ref, cos_ref, sin_ref,
                   sb_ref, z_ref, xbc_ref, dt_ref, qm_ref, km_ref, vm_ref):
    xn = _rms(h_ref[...], g_ref[...]).astype(BF16)

    def proj(lo, hi):
        return _dot(xn, w_ref[:, lo:hi])

    sb = proj(C_SB, C_Z)
    sb_ref[:, :SB_WIDTH] = (sb[:, :SB_WIDTH] * (SB_HEAD_DIM ** -0.5)).astype(BF16)
    sb_ref[:, SB_WIDTH:] = sb[:, SB_WIDTH:].astype(BF16)
    z_ref[...] = proj(C_Z, C_XBC)
    xbc_ref[...] = proj(C_XBC, C_CQ)
    dt_ref[...] = proj(C_DT, N_IN)

    cos = cos_ref[...]
    sin = sin_ref[...]

    def rope(x):
        return x * cos + pltpu.roll(x, LANES // 2, axis=1) * sin

    cq = _rms(proj(C_CQ, C_CKV), qn_ref[...]).astype(BF16)
    qf = _dot(cq, wuq_ref[...])
    ckv = _rms(proj(C_CKV, C_KPE), kvn_ref[...]).astype(BF16)
    kn = _dot(ckv, wk_ref[...])
    vm_ref[...] = _dot(ckv, wv_ref[...]).astype(BF16)
    kpe = rope(proj(C_KPE, C_DT))
    for h in range(MLA_HEADS):
        sl = slice(h * MLA_PAD, (h + 1) * MLA_PAD)
        qm_ref[:, sl] = (rope(qf[:, sl]) * MLA_Q_SCALE).astype(BF16)
        km_ref[:, sl] = (kn[:, sl] + kpe).astype(BF16)


def _mix_in(h, g, w_in_p, qn, wuq_p, kvn, wk_p, wv_p, cos_t, sin_t, tm):
    t = h.shape[0]
    row = lambda n: pl.BlockSpec((tm, n), lambda i: (i, 0))
    full = lambda a: pl.BlockSpec(a.shape, lambda i: (0, 0))
    out_widths = [(3 * SB_WIDTH, BF16), (SSM_INNER, F32), (SSM_CONV_DIM, F32), (LANES, F32),
                  (MLA_HEADS * MLA_PAD, BF16), (MLA_HEADS * MLA_PAD, BF16), (MLA_WIDTH, BF16)]
    return pl.pallas_call(
        _mix_in_kernel,
        grid=(t // tm,),
        in_specs=[row(D_MODEL), full(g), full(w_in_p), full(qn), full(wuq_p), full(kvn), full(wk_p),
                  full(wv_p), row(LANES), row(LANES)],
        out_specs=[row(n) for n, _ in out_widths],
        out_shape=[jax.ShapeDtypeStruct((t, n), d) for n, d in out_widths],
        compiler_params=pltpu.CompilerParams(dimension_semantics=("parallel",),
                                             vmem_limit_bytes=VMEM_LIMIT),
        name="mix_in",
    )(h, g, w_in_p, qn, wuq_p, kvn, wk_p, wv_p, cos_t, sin_t)


def _sb_kernel(q_ref, k_ref, v_ref, o_ref, c_ref, acc_ref, *, tq):
    tk = tq
    qi = pl.program_id(1)
    upper = (lax.broadcasted_iota(jnp.int32, (tk, tk), 0) >
             lax.broadcasted_iota(jnp.int32, (tk, tk), 1)).astype(BF16)
    lane = lax.broadcasted_iota(jnp.int32, (tq, LANES), 1)
    c_ref[...] = jnp.zeros(c_ref.shape, F32)
    acc_ref[...] = jnp.zeros(acc_ref.shape, F32)

    heads = range(SB_HEADS)
    pair = [slice((h // 2) * LANES, (h // 2 + 1) * LANES) for h in heads]

    def step(j, masked):
        ks = pl.multiple_of(j * tk, tk)
        z = []
        for h in heads:
            own = (lane < SB_HEAD_DIM) if h % 2 == 0 else (lane >= SB_HEAD_DIM)
            qh = jnp.where(own, q_ref[:, pair[h]], jnp.zeros((), BF16))
            z.append(_dot_nt(qh, k_ref[pl.ds(ks, tk), pair[h]]))
        log_keep = [-_softplus(z[h]) for h in heads]
        if masked:
            mask = (lax.broadcasted_iota(jnp.int32, (tq, tk), 1) <
                    lax.broadcasted_iota(jnp.int32, (tq, tk), 0))
            log_keep = [jnp.where(mask, x, 0.0) for x in log_keep]
        hi = [x.astype(BF16) for x in log_keep]
        lo = [(log_keep[h] - hi[h].astype(F32)).astype(BF16) for h in heads]
        tail = [_dot(hi[h], upper) + _dot(lo[h], upper) for h in heads]
        c_all = None
        pv = []
        for h in heads:
            c = c_ref[h]
            later = tail[h] + pltpu.repeat(c, tk // LANES, axis=1)
            w = jnp.exp(z[h] + log_keep[h] + later)
            if masked:
                w = jnp.where(mask, w, 0.0)
            pv.append(_dot(w.astype(BF16), v_ref[pl.ds(ks, tk), pair[h]]))
            c = c + jnp.sum(log_keep[h], axis=1, keepdims=True)
            c_ref[h] = c
            c_all = c if c_all is None else jnp.maximum(c_all, c)
        for h in heads:
            acc_ref[h] += pv[h]
        return jnp.max(c_all)

    def cond(carry):
        j, c_max = carry
        return jnp.logical_and(j >= 0, c_max > SB_LOG_FLOOR)

    def body(carry):
        j, _ = carry
        return j - 1, step(j, False)

    lax.while_loop(cond, body, (qi - 1, step(qi, True)))
    for hp in range(SB_HEADS // 2):
        o_ref[:, hp * LANES:(hp + 1) * LANES] = jnp.where(lane < SB_HEAD_DIM, acc_ref[2 * hp],
                                                          acc_ref[2 * hp + 1])


def _sb_attn(sb, b, s, tq):
    nq = s // tq
    return pl.pallas_call(
        functools.partial(_sb_kernel, tq=tq),
        grid=(b, nq),
        in_specs=[pl.BlockSpec((tq, SB_WIDTH), lambda bi, qi: (bi * nq + qi, 0)),
                  pl.BlockSpec((s, SB_WIDTH), lambda bi, qi: (bi, 1)),
                  pl.BlockSpec((s, SB_WIDTH), lambda bi, qi: (bi, 2))],
        out_specs=pl.BlockSpec((tq, SB_WIDTH), lambda bi, qi: (bi * nq + qi, 0)),
        out_shape=jax.ShapeDtypeStruct((b * s, SB_WIDTH), F32),
        scratch_shapes=[pltpu.VMEM((SB_HEADS, tq, LANES), F32),
                        pltpu.VMEM((SB_HEADS, tq, LANES), F32)],
        compiler_params=pltpu.CompilerParams(dimension_semantics=("parallel", "parallel"),
                                             vmem_limit_bytes=VMEM_LIMIT),
        name="sb_attn",
    )(sb, sb, sb)


def _ssd_kernel(xbc_ref, z_ref, dt_ref, cw_ref, cb_ref, dtb_ref, alog_ref, dl_ref, gn_ref,
                y_ref, tail_ref, xs_ref, state_ref):
    l = SSM_CHUNK

    @pl.when(pl.program_id(1) == 0)
    def _():
        tail_ref[...] = jnp.zeros_like(tail_ref)
        state_ref[...] = jnp.zeros_like(state_ref)

    xs_ref[0:SUBLANES, :] = tail_ref[...]
    xs_ref[SUBLANES:SUBLANES + l, :] = xbc_ref[...]
    tail_ref[...] = xbc_ref[l - SUBLANES:l, :]
    conv = cb_ref[...]
    for k in range(SSM_CONV):
        conv = conv + cw_ref[k:k + 1, :] * xs_ref[pl.ds(SUBLANES - (SSM_CONV - 1) + k, l), :]
    xa = _silu(conv)
    x = xa[:, :SSM_INNER]
    gn = SSM_GROUPS * SSM_STATE
    bm = xa[:, SSM_INNER:SSM_INNER + gn]
    cm = xa[:, SSM_INNER + gn:]

    dt = _softplus(dt_ref[...] + dtb_ref[...])
    da = dt * (-jnp.exp(alog_ref[...]))
    tril = (lax.broadcasted_iota(jnp.int32, (l, l), 0) >=
            lax.broadcasted_iota(jnp.int32, (l, l), 1))
    trilb = tril.astype(BF16)
    d1, d2, d3 = _split3(da)
    cs = _dot(trilb, d1) + _dot(trilb, d2) + _dot(trilb, d3)
    cs_t = cs.T
    cs_last = cs[l - 1:l, :]

    ys = []
    for g in range(SSM_GROUPS):
        bg = bm[:, g * SSM_STATE:(g + 1) * SSM_STATE]
        cg = cm[:, g * SSM_STATE:(g + 1) * SSM_STATE].astype(BF16)
        cb = _dot_nt(cg, bg.astype(BF16))
        for r in range(SSM_HEADS // SSM_GROUPS):
            h = g * (SSM_HEADS // SSM_GROUPS) + r
            hs = slice(h * SSM_HEAD_DIM, (h + 1) * SSM_HEAD_DIM)
            cs_col = cs[:, h:h + 1]
            decay = jnp.exp(jnp.where(tril, cs_col - cs_t[h:h + 1, :], -jnp.inf))
            xh = x[:, hs]
            xdt = (xh * dt[:, h:h + 1]).astype(BF16)
            y_diag = _dot((cb * decay).astype(BF16), xdt)
            st = state_ref[h]
            y_off = _dot(cg, st.astype(BF16)) * jnp.exp(cs_col)
            end = cs_last[:, h:h + 1]
            to_end = jnp.exp(end - cs_col)
            state_ref[h] = st * jnp.exp(end) + _dot_tn((bg * to_end).astype(BF16), xdt)
            ys.append(y_diag + y_off + xh * dl_ref[:, hs])
    y = jnp.concatenate(ys, axis=1) * _silu(z_ref[...])
    y_ref[...] = _rms(y, gn_ref[...]).astype(BF16)


def _ssd(xbc, z, dt, cw, cb, dtb, alog, dl, gn, b, s):
    l = SSM_CHUNK
    nc = s // l
    row = lambda n: pl.BlockSpec((l, n), lambda bi, ci: (bi * nc + ci, 0))
    full = lambda a: pl.BlockSpec(a.shape, lambda bi, ci: (0, 0))
    return pl.pallas_call(
        _ssd_kernel,
        grid=(b, nc),
        in_specs=[row(SSM_CONV_DIM), row(SSM_INNER), row(LANES), full(cw), full(cb), full(dtb),
                  full(alog), full(dl), full(gn)],
        out_specs=row(SSM_INNER),
        out_shape=jax.ShapeDtypeStruct((b * s, SSM_INNER), BF16),
        scratch_shapes=[pltpu.VMEM((SUBLANES, SSM_CONV_DIM), F32),
                        pltpu.VMEM((SUBLANES + l, SSM_CONV_DIM), F32),
                        pltpu.VMEM((SSM_HEADS, SSM_STATE, SSM_HEAD_DIM), F32)],
        compiler_params=pltpu.CompilerParams(dimension_semantics=("parallel", "arbitrary"),
                                             vmem_limit_bytes=VMEM_LIMIT),
        name="ssd",
    )(xbc, z, dt, cw, cb, dtb, alog, dl, gn)


def _mla_kernel(q_ref, k_ref, v_ref, o_ref, m_ref, l_ref, acc_ref, *, tq):
    tk = tq
    qi = pl.program_id(1)
    m_ref[...] = jnp.full(m_ref.shape, -1e30, F32)
    l_ref[...] = jnp.zeros(l_ref.shape, F32)
    acc_ref[...] = jnp.zeros(acc_ref.shape, F32)

    heads = range(MLA_HEADS)

    def step(j, masked):
        ks = pl.multiple_of(j * tk, tk)
        sc = []
        for h in heads:
            hs = slice(h * MLA_PAD, (h + 1) * MLA_PAD)
            sc.append(_dot_nt(q_ref[:, hs], k_ref[pl.ds(ks, tk), hs]))
        if masked:
            keep = (lax.broadcasted_iota(jnp.int32, (tq, tk), 1) <=
                    lax.broadcasted_iota(jnp.int32, (tq, tk), 0))
            sc = [jnp.where(keep, x, -jnp.inf) for x in sc]
        alpha, pv = [], []
        for h in heads:
            m_old = m_ref[h]
            m_new = jnp.maximum(m_old, jnp.max(sc[h], axis=1, keepdims=True))
            a = jnp.exp2(m_old - m_new)
            p = jnp.exp2(sc[h] - pltpu.repeat(m_new, tk // LANES, axis=1))
            l_ref[h] = a * l_ref[h] + jnp.sum(p, axis=1, keepdims=True)
            m_ref[h] = m_new
            vp = v_ref[pl.ds(ks, tk), (h // 2) * LANES:(h // 2 + 1) * LANES]
            alpha.append(a)
            pv.append(_dot(p.astype(BF16), vp))
        for h in heads:
            acc_ref[h] = alpha[h] * acc_ref[h] + pv[h]

    def body(j, carry):
        step(j, False)
        return carry

    lax.fori_loop(0, qi, body, 0)
    step(qi, True)
    lane = lax.broadcasted_iota(jnp.int32, (tq, LANES), 1)
    for hp in range(MLA_HEADS // 2):
        even = acc_ref[2 * hp] / l_ref[2 * hp]
        odd = acc_ref[2 * hp + 1] / l_ref[2 * hp + 1]
        o_ref[:, hp * LANES:(hp + 1) * LANES] = jnp.where(lane < MLA_V, even, odd)


def _mla_attn(qm, km, vm, b, s, tq):
    nq = s // tq
    return pl.pallas_call(
        functools.partial(_mla_kernel, tq=tq),
        grid=(b, nq),
        in_specs=[pl.BlockSpec((tq, MLA_HEADS * MLA_PAD), lambda bi, qi: (bi * nq + qi, 0)),
                  pl.BlockSpec((s, MLA_HEADS * MLA_PAD), lambda bi, qi: (bi, 0)),
                  pl.BlockSpec((s, MLA_WIDTH), lambda bi, qi: (bi, 0))],
        out_specs=pl.BlockSpec((tq, MLA_WIDTH), lambda bi, qi: (bi * nq + qi, 0)),
        out_shape=jax.ShapeDtypeStruct((b * s, MLA_WIDTH), F32),
        scratch_shapes=[pltpu.VMEM((MLA_HEADS, tq, LANES), F32),
                        pltpu.VMEM((MLA_HEADS, tq, LANES), F32),
                        pltpu.VMEM((MLA_HEADS, tq, LANES), F32)],
        compiler_params=pltpu.CompilerParams(dimension_semantics=("parallel", "parallel"),
                                             vmem_limit_bytes=VMEM_LIMIT),
        name="mla_attn",
    )(qm, km, vm)


def _mix_out_kernel(h_ref, ya_ref, yb_ref, yc_ref, ga_ref, gc_ref, w_ref, o_ref):
    ya = _rms(ya_ref[...], ga_ref[...]).astype(BF16)
    yc = _rms(yc_ref[...], gc_ref[...]).astype(BF16)
    c1 = SB_WIDTH
    c2 = SB_WIDTH + SSM_INNER
    o_ref[...] = (h_ref[...] + _dot(ya, w_ref[:c1, :]) + _dot(yb_ref[...], w_ref[c1:c2, :])
                  + _dot(yc, w_ref[c2:, :]))


def _mix_out(h, ya, yb, yc, ga, gc, w_out, tm):
    t = h.shape[0]
    row = lambda n: pl.BlockSpec((tm, n), lambda i: (i, 0))
    full = lambda a: pl.BlockSpec(a.shape, lambda i: (0, 0))
    return pl.pallas_call(
        _mix_out_kernel,
        grid=(t // tm,),
        in_specs=[row(D_MODEL), row(SB_WIDTH), row(SSM_INNER), row(MLA_WIDTH), full(ga), full(gc),
                  full(w_out)],
        out_specs=row(D_MODEL),
        out_shape=jax.ShapeDtypeStruct((t, D_MODEL), F32),
        compiler_params=pltpu.CompilerParams(dimension_semantics=("parallel",),
                                             vmem_limit_bytes=VMEM_LIMIT),
        name="mix_out",
    )(h, ya, yb, yc, ga, gc, w_out)


def _ffn_kernel(h_ref, g_ref, wg_ref, wv_ref, cwg_ref, cwv_ref, cbg_ref, cbv_ref, wd_ref, fin_ref,
                o_ref, hn_ref, acc_ref, carry_g, carry_v, sg_ref, sv_ref, *, tm, tiles_per_seq,
                final_norm):
    i = pl.program_id(0)
    j = pl.program_id(1)

    @pl.when(j == 0)
    def _():
        x = h_ref[...]
        hn_ref[...] = _rms(x, g_ref[...]).astype(BF16)
        acc_ref[...] = x

    first = (i % tiles_per_seq) == 0
    pad = SUBLANES

    def branch(w_ref, cw_ref, cb_ref, carry, s_ref):
        up = _dot(hn_ref[...], w_ref[...])
        s_ref[0:pad, :] = jnp.where(first, 0.0, carry[j])
        s_ref[pad:pad + tm, :] = up
        carry[j] = up[tm - pad:tm, :]
        return (cb_ref[...] + cw_ref[0:1, :] * s_ref[pl.ds(pad - 2, tm), :]
                + cw_ref[1:2, :] * s_ref[pl.ds(pad - 1, tm), :] + cw_ref[2:3, :] * up)

    gate = branch(wg_ref, cwg_ref, cbg_ref, carry_g, sg_ref)
    val = branch(wv_ref, cwv_ref, cbv_ref, carry_v, sv_ref)
    act = (_silu(gate) * val).astype(BF16)
    acc_ref[...] += _dot(act, wd_ref[...])

    @pl.when(j == pl.num_programs(1) - 1)
    def _():
        out = acc_ref[...]
        if final_norm:
            out = _rms(out, fin_ref[...])
        o_ref[...] = out


def _ffn(h, g, w_up, cw, cb, w_down, fin, s, tm, tf, final_norm):
    t = h.shape[0]
    nj = D_FF // tf
    full = lambda a: pl.BlockSpec(a.shape, lambda i, j: (0, 0))
    return pl.pallas_call(
        functools.partial(_ffn_kernel, tm=tm, tiles_per_seq=s // tm, final_norm=final_norm),
        grid=(t // tm, nj),
        in_specs=[pl.BlockSpec((tm, D_MODEL), lambda i, j: (i, 0)), full(g),
                  pl.BlockSpec((D_MODEL, tf), lambda i, j: (0, j)),
                  pl.BlockSpec((D_MODEL, tf), lambda i, j: (0, nj + j)),
                  pl.BlockSpec((FFN_CONV, tf), lambda i, j: (0, j)),
                  pl.BlockSpec((FFN_CONV, tf), lambda i, j: (0, nj + j)),
                  pl.BlockSpec((1, tf), lambda i, j: (0, j)),
                  pl.BlockSpec((1, tf), lambda i, j: (0, nj + j)),
                  pl.BlockSpec((tf, D_MODEL), lambda i, j: (j, 0)), full(fin)],
        out_specs=pl.BlockSpec((tm, D_MODEL), lambda i, j: (i, 0)),
        out_shape=jax.ShapeDtypeStruct((t, D_MODEL), F32),
        scratch_shapes=[pltpu.VMEM((tm, D_MODEL), BF16), pltpu.VMEM((tm, D_MODEL), F32),
                        pltpu.VMEM((nj, SUBLANES, tf), F32), pltpu.VMEM((nj, SUBLANES, tf), F32),
                        pltpu.VMEM((SUBLANES + tm, tf), F32), pltpu.VMEM((SUBLANES + tm, tf), F32)],
        compiler_params=pltpu.CompilerParams(dimension_semantics=("arbitrary", "arbitrary"),
                                             vmem_limit_bytes=VMEM_LIMIT),
        name="ffn",
    )(h, g, w_up, w_up, cw, cw, cb, cb, w_down, fin)


def _pack_in_proj(w):
    d = w.shape[0]
    o_dt = C_CQ
    o_cq = o_dt + SSM_HEADS
    o_kr = o_cq + MLA_Q_RANK + MLA_KV_RANK
    kr = w[:, o_kr:o_kr + MLA_ROPE]
    zc = lambda n: jnp.zeros((d, n), w.dtype)
    kpe = jnp.concatenate([zc(32), kr[:, :HALF], zc(16), zc(32), kr[:, HALF:], zc(16)], axis=1)
    dtb = jnp.concatenate([w[:, o_dt:o_cq], zc(LANES - SSM_HEADS)], axis=1)
    return jnp.concatenate([w[:, :o_dt], w[:, o_cq:o_kr], kpe, dtb], axis=1).astype(BF16)


def _pack_uq(w):
    r = w.shape[0]
    zc = lambda n: jnp.zeros((r, n), w.dtype)
    cols = []
    for h in range(MLA_HEADS):
        o = h * (MLA_NOPE + MLA_ROPE)
        nope = w[:, o:o + MLA_NOPE]
        rp = w[:, o + MLA_NOPE:o + MLA_NOPE + MLA_ROPE]
        cols += [nope[:, :32], rp[:, :HALF], zc(16), nope[:, 32:], rp[:, HALF:], zc(16)]
    return jnp.concatenate(cols, axis=1).astype(BF16)


def _pack_ukv(w):
    r = w.shape[0]
    zc = lambda n: jnp.zeros((r, n), w.dtype)
    kc, vc = [], []
    for h in range(MLA_HEADS):
        o = h * (MLA_NOPE + MLA_V)
        nope = w[:, o:o + MLA_NOPE]
        kc += [nope[:, :32], zc(32), nope[:, 32:], zc(32)]
        vc.append(w[:, o + MLA_NOPE:o + MLA_NOPE + MLA_V])
    return jnp.concatenate(kc, axis=1).astype(BF16), jnp.concatenate(vc, axis=1).astype(BF16)


def _lane_pad(v):
    return jnp.concatenate([v, jnp.zeros((LANES - v.shape[0],), v.dtype)])[None, :]


TM_IN = 256
TM_OUT = 512
TQ_SB = 256
TQ_MLA = 512
TM_FFN = 512
TF_FFN = 1408


def kernel(x, positions, mix_norm, w_in, sb_out_norm, ssm_conv_w, ssm_conv_b, ssm_dt_bias, ssm_a_log, ssm_d, ssm_out_norm, mla_q_norm, mla_w_uq, mla_kv_norm, mla_w_ukv, mla_out_norm, w_out, ffn_norm, ffn_w_up, ffn_conv_w, ffn_conv_b, ffn_w_down, final_norm):
    b, s, d = x.shape
    t = b * s
    depth = w_in.shape[0]
    h = x.reshape(t, d)

    inv_freq = 1.0 / (ROPE_THETA ** (jnp.arange(0, MLA_ROPE, 2, dtype=F32) / MLA_ROPE))
    z16, z32 = jnp.zeros((16,), F32), jnp.zeros((32,), F32)
    invf_lane = jnp.concatenate([z32, inv_freq, z16, z32, inv_freq, z16])[None, :]
    sign_lane = jnp.concatenate([z32, -jnp.ones((16,), F32), z16, z32, jnp.ones((16,), F32), z16])[None, :]
    cos_t, sin_t = _rope_tables(positions.reshape(t, 1), invf_lane, sign_lane, TM_IN)

    for i in range(depth):
        wk_p, wv_p = _pack_ukv(mla_w_ukv[i])
        sb, z, xbc, dt, qm, km, vm = _mix_in(
            h, mix_norm[i][None, :], _pack_in_proj(w_in[i]), mla_q_norm[i][None, :],
            _pack_uq(mla_w_uq[i]), mla_kv_norm[i][None, :], wk_p, wv_p, cos_t, sin_t, TM_IN)
        ya = _sb_attn(sb, b, s, TQ_SB)
        yb = _ssd(xbc, z, dt, ssm_conv_w[i], ssm_conv_b[i][None, :], _lane_pad(ssm_dt_bias[i]),
                  _lane_pad(ssm_a_log[i]), jnp.repeat(ssm_d[i], SSM_HEAD_DIM)[None, :],
                  ssm_out_norm[i][None, :], b, s)
        yc = _mla_attn(qm, km, vm, b, s, TQ_MLA)
        h = _mix_out(h, ya, yb, yc, sb_out_norm[i][None, :], mla_out_norm[i][None, :],
                     w_out[i].astype(BF16), TM_OUT)
        h = _ffn(h, ffn_norm[i][None, :], ffn_w_up[i].astype(BF16), ffn_conv_w[i],
                 ffn_conv_b[i][None, :], ffn_w_down[i].astype(BF16), final_norm[None, :], s,
                 TM_FFN, TF_FFN, final_norm=(i == depth - 1))
    return h.reshape(b, s, d)
```

```python
import functools

import jax
import jax.numpy as jnp
from jax import lax
from jax.experimental import pallas as pl
from jax.experimental.pallas import tpu as pltpu

F32 = jnp.float32
BF16 = jnp.bfloat16

D_MODEL = 1024
EPS = 1e-6
SB_HEADS = 4
SB_HEAD_DIM = 64
SB_WIDTH = SB_HEADS * SB_HEAD_DIM
SSM_HEADS = 8
SSM_HEAD_DIM = 64
SSM_INNER = SSM_HEADS * SSM_HEAD_DIM
SSM_GROUPS = 2
SSM_STATE = 64
SSM_CONV = 4
SSM_CHUNK = 128
SSM_CONV_DIM = SSM_INNER + 2 * SSM_GROUPS * SSM_STATE
MLA_HEADS = 4
MLA_NOPE = 64
MLA_ROPE = 32
MLA_V = 64
MLA_Q_RANK = 256
MLA_KV_RANK = 128
MLA_WIDTH = MLA_HEADS * MLA_V
ROPE_THETA = 10000.0
D_MIX = SB_WIDTH + SSM_INNER + MLA_WIDTH
D_FF = 2816
FFN_CONV = 3

LANES = 128
SUBLANES = 8
MLA_PAD = 128
HALF = MLA_ROPE // 2
VMEM_LIMIT = 48 * 1024 * 1024
VMEM_LIMIT_FFN = 62 * 1024 * 1024

C_SB = 0
C_Z = C_SB + 3 * SB_WIDTH
C_XBC = C_Z + SSM_INNER
C_CQ = C_XBC + SSM_CONV_DIM
C_CKV = C_CQ + MLA_Q_RANK
C_KPE = C_CKV + MLA_KV_RANK
C_DT = C_KPE + MLA_PAD
N_IN = C_DT + LANES

MLA_Q_SCALE = (MLA_NOPE + MLA_ROPE) ** -0.5 * 1.4426950408889634

SB_LOG_FLOOR = -104.0


def _rms(x, g):
    return x * lax.rsqrt(jnp.mean(x * x, axis=-1, keepdims=True) + EPS) * g


def _silu(x):
    return x * (1.0 / (1.0 + jnp.exp(-x)))


def _softplus(x):
    return jnp.maximum(x, 0.0) + jnp.log1p(jnp.exp(-jnp.abs(x)))


def _split3(x):
    a = x.astype(BF16)
    r = x - a.astype(F32)
    b = r.astype(BF16)
    c = (r - b.astype(F32)).astype(BF16)
    return a, b, c


def _lane_tile(x, n):
    return jnp.concatenate([x] * n, axis=1)


def _dot(a, b):
    return jnp.dot(a, b, preferred_element_type=F32)


def _dot_nt(a, b):
    return lax.dot_general(a, b, (((1,), (1,)), ((), ())), preferred_element_type=F32)


def _dot_tn(a, b):
    return lax.dot_general(a, b, (((0,), (0,)), ((), ())), preferred_element_type=F32)


def _rope_table_kernel(pos_ref, invf_ref, sign_ref, cos_ref, sin_ref):
    ang = pos_ref[...].astype(F32) * invf_ref[...]
    cos_ref[...] = jnp.cos(ang)
    sin_ref[...] = jnp.sin(ang) * sign_ref[...]


def _rope_tables(pos_col, invf_lane, sign_lane, tm):
    t = pos_col.shape[0]
    return pl.pallas_call(
        _rope_table_kernel,
        grid=(t // tm,),
        in_specs=[pl.BlockSpec((tm, 1), lambda i: (i, 0)),
                  pl.BlockSpec((1, LANES), lambda i: (0, 0)),
                  pl.BlockSpec((1, LANES), lambda i: (0, 0))],
        out_specs=[pl.BlockSpec((tm, LANES), lambda i: (i, 0)),
                   pl.BlockSpec((tm, LANES), lambda i: (i, 0))],
        out_shape=[jax.ShapeDtypeStruct((t, LANES), F32)] * 2,
        compiler_params=pltpu.CompilerParams(dimension_semantics=("parallel",)),
        name="rope_tables",
    )(pos_col, invf_lane, sign_lane)


def _mix_in_kernel(h_ref, g_ref, w_ref, qn_ref, wuq_ref, kvn_ref, wk_ref, wv_ref, cos_ref, sin_ref,
                   sb_ref, z_ref, xbc_ref, dt_ref, qm_ref, km_ref, vm_ref):
    xn = _rms(h_ref[...], g_ref[...]).astype(BF16)

    def proj(lo, hi):
        return _dot(xn, w_ref[:, lo:hi])

    sb = proj(C_SB, C_Z)
    sb_ref[:, :SB_WIDTH] = (sb[:, :SB_WIDTH] * (SB_HEAD_DIM ** -0.5)).astype(BF16)
    sb_ref[:, SB_WIDTH:] = sb[:, SB_WIDTH:].astype(BF16)
    z_ref[...] = proj(C_Z, C_XBC)
    xbc_ref[...] = proj(C_XBC, C_CQ)
    dt_ref[...] = proj(C_DT, N_IN)

    cos = cos_ref[...]
    sin = sin_ref[...]

    def rope(x):
        return x * cos + pltpu.roll(x, LANES // 2, axis=1) * sin

    cq = _rms(proj(C_CQ, C_CKV), qn_ref[...]).astype(BF16)
    qf = _dot(cq, wuq_ref[...])
    ckv = _rms(proj(C_CKV, C_KPE), kvn_ref[...]).astype(BF16)
    kn = _dot(ckv, wk_ref[...])
    vm_ref[...] = _dot(ckv, wv_ref[...]).astype(BF16)
    kpe = rope(proj(C_KPE, C_DT))
    for h in range(MLA_HEADS):
        sl = slice(h * MLA_PAD, (h + 1) * MLA_PAD)
        qm_ref[:, sl] = (rope(qf[:, sl]) * MLA_Q_SCALE).astype(BF16)
        km_ref[:, sl] = (kn[:, sl] + kpe).astype(BF16)


def _mix_in(h, g, w_in_p, qn, wuq_p, kvn, wk_p, wv_p, cos_t, sin_t, tm):
    t = h.shape[0]
    row = lambda n: pl.BlockSpec((tm, n), lambda i: (i, 0))
    full = lambda a: pl.BlockSpec(a.shape, lambda i: (0, 0))
    out_widths = [(3 * SB_WIDTH, BF16), (SSM_INNER, F32), (SSM_CONV_DIM, F32), (LANES, F32),
                  (MLA_HEADS * MLA_PAD, BF16), (MLA_HEADS * MLA_PAD, BF16), (MLA_WIDTH, BF16)]
    return pl.pallas_call(
        _mix_in_kernel,
        grid=(t // tm,),
        in_specs=[row(D_MODEL), full(g), full(w_in_p), full(qn), full(wuq_p), full(kvn), full(wk_p),
                  full(wv_p), row(LANES), row(LANES)],
        out_specs=[row(n) for n, _ in out_widths],
        out_shape=[jax.ShapeDtypeStruct((t, n), d) for n, d in out_widths],
        compiler_params=pltpu.CompilerParams(dimension_semantics=("parallel",),
                                             vmem_limit_bytes=VMEM_LIMIT),
        name="mix_in",
    )(h, g, w_in_p, qn, wuq_p, kvn, wk_p, wv_p, cos_t, sin_t)


def _sb_kernel(q_ref, k_ref, v_ref, o_ref, c_ref, acc_ref, *, tq):
    tk = tq
    qi = pl.program_id(1)
    upper = (lax.broadcasted_iota(jnp.int32, (tk, tk), 0) >
             lax.broadcasted_iota(jnp.int32, (tk, tk), 1)).astype(BF16)
    lane = lax.broadcasted_iota(jnp.int32, (tq, LANES), 1)
    c_ref[...] = jnp.zeros(c_ref.shape, F32)
    acc_ref[...] = jnp.zeros(acc_ref.shape, F32)

    heads = range(SB_HEADS)
    pair = [slice((h // 2) * LANES, (h // 2 + 1) * LANES) for h in heads]

    def step(j, masked):
        ks = pl.multiple_of(j * tk, tk)
        z = []
        for h in heads:
            own = (lane < SB_HEAD_DIM) if h % 2 == 0 else (lane >= SB_HEAD_DIM)
            qh = jnp.where(own, q_ref[:, pair[h]], jnp.zeros((), BF16))
            z.append(_dot_nt(qh, k_ref[pl.ds(ks, tk), pair[h]]))
        log_keep = [-_softplus(z[h]) for h in heads]
        if masked:
            mask = (lax.broadcasted_iota(jnp.int32, (tq, tk), 1) <
                    lax.broadcasted_iota(jnp.int32, (tq, tk), 0))
            log_keep = [jnp.where(mask, x, 0.0) for x in log_keep]
        hi = [x.astype(BF16) for x in log_keep]
        lo = [(log_keep[h] - hi[h].astype(F32)).astype(BF16) for h in heads]
        tail = [_dot(hi[h], upper) + _dot(lo[h], upper) for h in heads]
        c_all = None
        pv = []
        for h in heads:
            c = c_ref[h]
            later = tail[h] + _lane_tile(c, tk // LANES)
            w = jnp.exp(z[h] + log_keep[h] + later)
            if masked:
                w = jnp.where(mask, w, 0.0)
            pv.append(_dot(w.astype(BF16), v_ref[pl.ds(ks, tk), pair[h]]))
            c = c + jnp.sum(log_keep[h], axis=1, keepdims=True)
            c_ref[h] = c
            c_all = c if c_all is None else jnp.maximum(c_all, c)
        for h in heads:
            acc_ref[h] += pv[h]
        return jnp.max(c_all)

    def cond(carry):
        j, c_max = carry
        return jnp.logical_and(j >= 0, c_max > SB_LOG_FLOOR)

    def body(carry):
        j, _ = carry
        return j - 1, step(j, False)

    lax.while_loop(cond, body, (qi - 1, step(qi, True)))
    for hp in range(SB_HEADS // 2):
        o_ref[:, hp * LANES:(hp + 1) * LANES] = jnp.where(lane < SB_HEAD_DIM, acc_ref[2 * hp],
                                                          acc_ref[2 * hp + 1])


def _sb_attn(sb, b, s, tq):
    nq = s // tq
    return pl.pallas_call(
        functools.partial(_sb_kernel, tq=tq),
        grid=(b, nq),
        in_specs=[pl.BlockSpec((tq, SB_WIDTH), lambda bi, qi: (bi * nq + qi, 0)),
                  pl.BlockSpec((s, SB_WIDTH), lambda bi, qi: (bi, 1)),
                  pl.BlockSpec((s, SB_WIDTH), lambda bi, qi: (bi, 2))],
        out_specs=pl.BlockSpec((tq, SB_WIDTH), lambda bi, qi: (bi * nq + qi, 0)),
        out_shape=jax.ShapeDtypeStruct((b * s, SB_WIDTH), F32),
        scratch_shapes=[pltpu.VMEM((SB_HEADS, tq, LANES), F32),
                        pltpu.VMEM((SB_HEADS, tq, LANES), F32)],
        compiler_params=pltpu.CompilerParams(dimension_semantics=("parallel", "parallel"),
                                             vmem_limit_bytes=VMEM_LIMIT),
        name="sb_attn",
    )(sb, sb, sb)


def _ssd_kernel(xbc_ref, z_ref, dt_ref, cw_ref, cb_ref, dtb_ref, alog_ref, dl_ref, gn_ref,
                y_ref, tail_ref, xs_ref, state_ref, *, nch):
    l = SSM_CHUNK
    rows = nch * l
    hd = SSM_HEAD_DIM
    pairs = SSM_HEADS // 2

    @pl.when(pl.program_id(1) == 0)
    def _():
        tail_ref[...] = jnp.zeros_like(tail_ref)
        state_ref[...] = jnp.zeros_like(state_ref)

    xs_ref[0:SUBLANES, :] = tail_ref[...]
    xs_ref[SUBLANES:SUBLANES + rows, :] = xbc_ref[...]
    tail_ref[...] = xbc_ref[rows - SUBLANES:rows, :]
    conv = cb_ref[...]
    for k in range(SSM_CONV):
        conv = conv + cw_ref[k:k + 1, :] * xs_ref[pl.ds(SUBLANES - (SSM_CONV - 1) + k, rows), :]
    xa = _silu(conv)
    dt_all = _softplus(dt_ref[...] + dtb_ref[...])
    da_all = dt_all * (-jnp.exp(alog_ref[...]))

    tril = (lax.broadcasted_iota(jnp.int32, (l, l), 0) >=
            lax.broadcasted_iota(jnp.int32, (l, l), 1))
    trilb = tril.astype(BF16)
    lo_half = lax.broadcasted_iota(jnp.int32, (l, LANES), 1) < hd
    lo_half_s = lax.broadcasted_iota(jnp.int32, (SSM_STATE, LANES), 1) < hd
    gn = SSM_GROUPS * SSM_STATE

    for c in range(nch):
        rs = slice(c * l, (c + 1) * l)
        x = xa[rs, :SSM_INNER]
        xb = x.astype(BF16)
        bm = xa[rs, SSM_INNER:SSM_INNER + gn]
        cm = xa[rs, SSM_INNER + gn:]
        d1, d2, d3 = _split3(da_all[rs])
        cs = _dot(trilb, d1) + _dot(trilb, d2) + _dot(trilb, d3)
        cs8 = cs.T[0:SSM_HEADS]
        dt8 = dt_all[rs].T[0:SSM_HEADS]
        bm_t = bm.T
        end8 = cs8[:, l - 1:l]
        w8 = jnp.exp(end8 - cs8) * dt8
        dec8 = jnp.broadcast_to(jnp.exp(end8), (SSM_HEADS, LANES))
        cg = [cm[:, g * SSM_STATE:(g + 1) * SSM_STATE].astype(BF16) for g in range(SSM_GROUPS)]
        bg_t = [bm_t[g * SSM_STATE:(g + 1) * SSM_STATE] for g in range(SSM_GROUPS)]
        cb = [_dot(cg[g], bg_t[g].astype(BF16)) for g in range(SSM_GROUPS)]

        ys = []
        for p in range(pairs):
            g = (2 * p) // (SSM_HEADS // SSM_GROUPS)
            ps = slice(p * LANES, (p + 1) * LANES)
            xp = xb[:, ps]
            y_diag, contrib, col = [], [], []
            for h in (2 * p, 2 * p + 1):
                cs_col = jnp.broadcast_to(cs[:, h:h + 1], (l, l))
                decay = jnp.exp(cs_col - cs8[h:h + 1, :])
                m = jnp.where(tril, cb[g] * decay * dt8[h:h + 1, :], 0.0)
                y_diag.append(_dot(m.astype(BF16), xp))
                contrib.append(_dot((bg_t[g] * w8[h:h + 1, :]).astype(BF16), xp))
                col.append(cs_col)
            st = state_ref[p]
            y_off = _dot(cg[g], st.astype(BF16)) * jnp.exp(jnp.where(lo_half, col[0], col[1]))
            st_decay = jnp.where(lo_half_s, dec8[2 * p:2 * p + 1, :], dec8[2 * p + 1:2 * p + 2, :])
            state_ref[p] = st * st_decay + jnp.where(lo_half_s, contrib[0], contrib[1])
            ys.append(jnp.where(lo_half, y_diag[0], y_diag[1]) + y_off + x[:, ps] * dl_ref[:, ps])
        y = jnp.concatenate(ys, axis=1) * _silu(z_ref[rs, :])
        y_ref[rs, :] = _rms(y, gn_ref[...]).astype(BF16)


def _ssd(xbc, z, dt, cw, cb, dtb, alog, dl, gn, b, s, nch):
    rows = nch * SSM_CHUNK
    nb = s // rows
    row = lambda n: pl.BlockSpec((rows, n), lambda bi, ci: (bi * nb + ci, 0))
    full = lambda a: pl.BlockSpec(a.shape, lambda bi, ci: (0, 0))
    return pl.pallas_call(
        functools.partial(_ssd_kernel, nch=nch),
        grid=(b, nb),
        in_specs=[row(SSM_CONV_DIM), row(SSM_INNER), row(LANES), full(cw), full(cb), full(dtb),
                  full(alog), full(dl), full(gn)],
        out_specs=row(SSM_INNER),
        out_shape=jax.ShapeDtypeStruct((b * s, SSM_INNER), BF16),
        scratch_shapes=[pltpu.VMEM((SUBLANES, SSM_CONV_DIM), F32),
                        pltpu.VMEM((SUBLANES + rows, SSM_CONV_DIM), F32),
                        pltpu.VMEM((SSM_HEADS // 2, SSM_STATE, LANES), F32)],
        compiler_params=pltpu.CompilerParams(dimension_semantics=("parallel", "arbitrary"),
                                             vmem_limit_bytes=VMEM_LIMIT),
        name="ssd",
    )(xbc, z, dt, cw, cb, dtb, alog, dl, gn)


def _mla_kernel(q_ref, k_ref, v_ref, o_ref, m_ref, l_ref, acc_ref, *, tq, tk):
    qi = pl.program_id(1)
    diag_tiles = tq // tk
    m_ref[...] = jnp.full(m_ref.shape, -1e30, F32)
    l_ref[...] = jnp.zeros(l_ref.shape, F32)
    acc_ref[...] = jnp.zeros(acc_ref.shape, F32)

    heads = range(MLA_HEADS)

    def step(j, diag):
        ks = pl.multiple_of(j * tk, tk)
        sc = []
        for h in heads:
            hs = slice(h * MLA_PAD, (h + 1) * MLA_PAD)
            sc.append(_dot_nt(q_ref[:, hs], k_ref[pl.ds(ks, tk), hs]))
        if diag is not None:
            keep = (lax.broadcasted_iota(jnp.int32, (tq, tk), 1) + diag * tk <=
                    lax.broadcasted_iota(jnp.int32, (tq, tk), 0))
            sc = [jnp.where(keep, x, -jnp.inf) for x in sc]
        alpha, pv = [], []
        for h in heads:
            m_old = m_ref[h]
            m_new = jnp.maximum(m_old, jnp.max(sc[h], axis=1, keepdims=True))
            a = jnp.exp2(m_old - m_new)
            p = jnp.exp2(sc[h] - _lane_tile(m_new, tk // LANES))
            l_ref[h] = a * l_ref[h] + jnp.sum(p, axis=1, keepdims=True)
            m_ref[h] = m_new
            vp = v_ref[pl.ds(ks, tk), (h // 2) * LANES:(h // 2 + 1) * LANES]
            alpha.append(a)
            pv.append(_dot(p.astype(BF16), vp))
        for h in heads:
            acc_ref[h] = alpha[h] * acc_ref[h] + pv[h]

    def body(j, carry):
        step(j, None)
        return carry

    lax.fori_loop(0, qi * diag_tiles, body, 0)
    for u in range(diag_tiles):
        step(qi * diag_tiles + u, u)
    lane = lax.broadcasted_iota(jnp.int32, (tq, LANES), 1)
    for hp in range(MLA_HEADS // 2):
        even = acc_ref[2 * hp] / l_ref[2 * hp]
        odd = acc_ref[2 * hp + 1] / l_ref[2 * hp + 1]
        o_ref[:, hp * LANES:(hp + 1) * LANES] = jnp.where(lane < MLA_V, even, odd)


def _mla_attn(qm, km, vm, b, s, tq, tk):
    nq = s // tq
    return pl.pallas_call(
        functools.partial(_mla_kernel, tq=tq, tk=tk),
        grid=(b, nq),
        in_specs=[pl.BlockSpec((tq, MLA_HEADS * MLA_PAD), lambda bi, qi: (bi * nq + qi, 0)),
                  pl.BlockSpec((s, MLA_HEADS * MLA_PAD), lambda bi, qi: (bi, 0),
                               pipeline_mode=pl.Buffered(1)),
                  pl.BlockSpec((s, MLA_WIDTH), lambda bi, qi: (bi, 0),
                               pipeline_mode=pl.Buffered(1))],
        out_specs=pl.BlockSpec((tq, MLA_WIDTH), lambda bi, qi: (bi * nq + qi, 0)),
        out_shape=jax.ShapeDtypeStruct((b * s, MLA_WIDTH), F32),
        scratch_shapes=[pltpu.VMEM((MLA_HEADS, tq, LANES), F32),
                        pltpu.VMEM((MLA_HEADS, tq, LANES), F32),
                        pltpu.VMEM((MLA_HEADS, tq, LANES), F32)],
        compiler_params=pltpu.CompilerParams(dimension_semantics=("parallel", "parallel"),
                                             vmem_limit_bytes=VMEM_LIMIT),
        name="mla_attn",
    )(qm, km, vm)


def _out_ffn_kernel(h_ref, ya_ref, yb_ref, yc_ref, ga_ref, gc_ref, wo_ref, g_ref, wg_ref, wv_ref,
                    cwg_ref, cwv_ref, cbg_ref, cbv_ref, wd_ref, fin_ref,
                    o_ref, hn_ref, acc_ref, carry_g, carry_v, sg0, sg1, sv0, sv1, act0, act1, *, tm,
                    tiles_per_seq, final_norm):
    ya = _rms(ya_ref[...], ga_ref[...]).astype(BF16)
    yc = _rms(yc_ref[...], gc_ref[...]).astype(BF16)
    c1 = SB_WIDTH
    c2 = SB_WIDTH + SSM_INNER
    h1 = (h_ref[...] + _dot(ya, wo_ref[:c1, :]) + _dot(yb_ref[...], wo_ref[c1:c2, :])
          + _dot(yc, wo_ref[c2:, :]))
    hn_ref[...] = _rms(h1, g_ref[...]).astype(BF16)
    acc_ref[...] = h1

    first = (pl.program_id(0) % tiles_per_seq) == 0
    pad = SUBLANES
    nj = wg_ref.shape[0]

    sg, sv, act = (sg0, sg1), (sv0, sv1), (act0, act1)

    def up_proj(j, slot):
        for w_ref, carry_ref, s_ref in ((wg_ref, carry_g, sg[slot]), (wv_ref, carry_v, sv[slot])):
            up = _dot(hn_ref[...], w_ref[j])
            s_ref[0:pad, :] = jnp.where(first, 0.0, carry_ref[j])
            s_ref[pad:pad + tm, :] = up
            carry_ref[j] = up[tm - pad:tm, :]

    def conv_act(j, slot):
        def conv(cw_ref, cb_ref, s_ref):
            cw = cw_ref[j]
            return (cb_ref[j] + cw[0:1, :] * s_ref[pl.ds(pad - 2, tm), :]
                    + cw[1:2, :] * s_ref[pl.ds(pad - 1, tm), :]
                    + cw[2:3, :] * s_ref[pl.ds(pad, tm), :])

        act[slot][...] = (_silu(conv(cwg_ref, cbg_ref, sg[slot]))
                          * conv(cwv_ref, cbv_ref, sv[slot])).astype(BF16)

    def down_proj(j, slot):
        acc_ref[...] += _dot(act[slot][...], wd_ref[j])

    def steady(j, slot):
        up_proj(j + 1, 1 - slot)
        conv_act(j, slot)
        down_proj(j, slot)

    assert nj % 2 == 1
    up_proj(0, 0)

    def pair(p, carry):
        steady(2 * p, 0)
        steady(2 * p + 1, 1)
        return carry

    lax.fori_loop(0, (nj - 1) // 2, pair, 0)
    conv_act(nj - 1, 0)
    down_proj(nj - 1, 0)
    out = acc_ref[...]
    if final_norm:
        out = _rms(out, fin_ref[...])
    o_ref[...] = out


def _out_ffn(h, ya, yb, yc, ga, gc, w_out, g, wg, wv, cwg, cwv, cbg, cbv, wd, fin, s, tm, final_norm):
    t = h.shape[0]
    nj, _, tf = wg.shape
    row = lambda n: pl.BlockSpec((tm, n), lambda i: (i, 0))

    def resident(a):
        return pl.BlockSpec(a.shape, lambda i: (0,) * a.ndim, pipeline_mode=pl.Buffered(1))

    return pl.pallas_call(
        functools.partial(_out_ffn_kernel, tm=tm, tiles_per_seq=s // tm, final_norm=final_norm),
        grid=(t // tm,),
        in_specs=[row(D_MODEL), row(SB_WIDTH), row(SSM_INNER), row(MLA_WIDTH)]
                 + [resident(a) for a in (ga, gc, w_out, g, wg, wv, cwg, cwv, cbg, cbv, wd, fin)],
        out_specs=row(D_MODEL),
        out_shape=jax.ShapeDtypeStruct((t, D_MODEL), F32),
        scratch_shapes=[pltpu.VMEM((tm, D_MODEL), BF16), pltpu.VMEM((tm, D_MODEL), F32),
                        pltpu.VMEM((nj, SUBLANES, tf), F32), pltpu.VMEM((nj, SUBLANES, tf), F32),
                        ] + [pltpu.VMEM((SUBLANES + tm, tf), F32)] * 4 + [pltpu.VMEM((tm, tf), BF16)] * 2,
        compiler_params=pltpu.CompilerParams(
            dimension_semantics=("arbitrary",), vmem_limit_bytes=VMEM_LIMIT_FFN),
        name="out_ffn",
    )(h, ya, yb, yc, ga, gc, w_out, g, wg, wv, cwg, cwv, cbg, cbv, wd, fin)


def _chunk_cols(a, tf):
    r, n = a.shape
    return a.reshape(r, n // tf, tf).transpose(1, 0, 2)


def _pack_in_proj(w):
    d = w.shape[0]
    o_dt = C_CQ
    o_cq = o_dt + SSM_HEADS
    o_kr = o_cq + MLA_Q_RANK + MLA_KV_RANK
    kr = w[:, o_kr:o_kr + MLA_ROPE]
    zc = lambda n: jnp.zeros((d, n), w.dtype)
    kpe = jnp.concatenate([zc(32), kr[:, :HALF], zc(16), zc(32), kr[:, HALF:], zc(16)], axis=1)
    dtb = jnp.concatenate([w[:, o_dt:o_cq], zc(LANES - SSM_HEADS)], axis=1)
    return jnp.concatenate([w[:, :o_dt], w[:, o_cq:o_kr], kpe, dtb], axis=1).astype(BF16)


def _pack_uq(w):
    r = w.shape[0]
    zc = lambda n: jnp.zeros((r, n), w.dtype)
    cols = []
    for h in range(MLA_HEADS):
        o = h * (MLA_NOPE + MLA_ROPE)
        nope = w[:, o:o + MLA_NOPE]
        rp = w[:, o + MLA_NOPE:o + MLA_NOPE + MLA_ROPE]
        cols += [nope[:, :32], rp[:, :HALF], zc(16), nope[:, 32:], rp[:, HALF:], zc(16)]
    return jnp.concatenate(cols, axis=1).astype(BF16)


def _pack_ukv(w):
    r = w.shape[0]
    zc = lambda n: jnp.zeros((r, n), w.dtype)
    kc, vc = [], []
    for h in range(MLA_HEADS):
        o = h * (MLA_NOPE + MLA_V)
        nope = w[:, o:o + MLA_NOPE]
        kc += [nope[:, :32], zc(32), nope[:, 32:], zc(32)]
        vc.append(w[:, o + MLA_NOPE:o + MLA_NOPE + MLA_V])
    return jnp.concatenate(kc, axis=1).astype(BF16), jnp.concatenate(vc, axis=1).astype(BF16)


def _lane_pad(v):
    return jnp.concatenate([v, jnp.zeros((LANES - v.shape[0],), v.dtype)])[None, :]


TM_IN = 512
TQ_SB = 256
TQ_MLA = 1024
TK_MLA = 512
SSD_BLOCK_CHUNKS = 4
TM_FFN = 1024
TF_FFN = 256


def kernel(x, positions, mix_norm, w_in, sb_out_norm, ssm_conv_w, ssm_conv_b, ssm_dt_bias, ssm_a_log, ssm_d, ssm_out_norm, mla_q_norm, mla_w_uq, mla_kv_norm, mla_w_ukv, mla_out_norm, w_out, ffn_norm, ffn_w_up, ffn_conv_w, ffn_conv_b, ffn_w_down, final_norm):
    b, s, d = x.shape
    t = b * s
    depth = w_in.shape[0]
    h = x.reshape(t, d)

    inv_freq = 1.0 / (ROPE_THETA ** (jnp.arange(0, MLA_ROPE, 2, dtype=F32) / MLA_ROPE))
    z16, z32 = jnp.zeros((16,), F32), jnp.zeros((32,), F32)
    invf_lane = jnp.concatenate([z32, inv_freq, z16, z32, inv_freq, z16])[None, :]
    sign_lane = jnp.concatenate([z32, -jnp.ones((16,), F32), z16, z32, jnp.ones((16,), F32), z16])[None, :]
    cos_t, sin_t = _rope_tables(positions.reshape(t, 1), invf_lane, sign_lane, TM_IN)

    for i in range(depth):
        wk_p, wv_p = _pack_ukv(mla_w_ukv[i])
        sb, z, xbc, dt, qm, km, vm = _mix_in(
            h, mix_norm[i][None, :], _pack_in_proj(w_in[i]), mla_q_norm[i][None, :],
            _pack_uq(mla_w_uq[i]), mla_kv_norm[i][None, :], wk_p, wv_p, cos_t, sin_t, TM_IN)
        ya = _sb_attn(sb, b, s, TQ_SB)
        yb = _ssd(xbc, z, dt, ssm_conv_w[i], ssm_conv_b[i][None, :], _lane_pad(ssm_dt_bias[i]),
                  _lane_pad(ssm_a_log[i]), jnp.repeat(ssm_d[i], SSM_HEAD_DIM)[None, :],
                  ssm_out_norm[i][None, :], b, s, SSD_BLOCK_CHUNKS)
        yc = _mla_attn(qm, km, vm, b, s, TQ_MLA, TK_MLA)
        w_up = ffn_w_up[i].astype(BF16)
        cw, cb = ffn_conv_w[i], ffn_conv_b[i][None, :]
        h = _out_ffn(h, ya, yb, yc, sb_out_norm[i][None, :], mla_out_norm[i][None, :],
                     w_out[i].astype(BF16), ffn_norm[i][None, :],
                     _chunk_cols(w_up[:, :D_FF], TF_FFN), _chunk_cols(w_up[:, D_FF:], TF_FFN),
                     _chunk_cols(cw[:, :D_FF], TF_FFN), _chunk_cols(cw[:, D_FF:], TF_FFN),
                     _chunk_cols(cb[:, :D_FF], TF_FFN), _chunk_cols(cb[:, D_FF:], TF_FFN),
                     ffn_w_down[i].astype(BF16).reshape(D_FF // TF_FFN, TF_FFN, D_MODEL),
                     final_norm[None, :], s, TM_FFN, final_norm=(i == depth - 1))
    return h.reshape(b, s, d)
```

```python
import functools

import jax
import jax.numpy as jnp
from jax import lax
from jax.experimental import pallas as pl
from jax.experimental.pallas import tpu as pltpu

F32 = jnp.float32
BF16 = jnp.bfloat16

D_MODEL = 1024
EPS = 1e-6
SB_HEADS = 4
SB_HEAD_DIM = 64
SB_WIDTH = SB_HEADS * SB_HEAD_DIM
SSM_HEADS = 8
SSM_HEAD_DIM = 64
SSM_INNER = SSM_HEADS * SSM_HEAD_DIM
SSM_GROUPS = 2
SSM_STATE = 64
SSM_CONV = 4
SSM_CHUNK = 128
SSM_CONV_DIM = SSM_INNER + 2 * SSM_GROUPS * SSM_STATE
MLA_HEADS = 4
MLA_NOPE = 64
MLA_ROPE = 32
MLA_V = 64
MLA_Q_RANK = 256
MLA_KV_RANK = 128
MLA_WIDTH = MLA_HEADS * MLA_V
ROPE_THETA = 10000.0
D_MIX = SB_WIDTH + SSM_INNER + MLA_WIDTH
D_FF = 2816
FFN_CONV = 3

LANES = 128
SUBLANES = 8
MLA_PAD = 128
HALF = MLA_ROPE // 2
VMEM_LIMIT = 48 * 1024 * 1024
VMEM_LIMIT_FFN = 62 * 1024 * 1024

C_SB = 0
C_Z = C_SB + 3 * SB_WIDTH
C_XBC = C_Z + SSM_INNER
C_CQ = C_XBC + SSM_CONV_DIM
C_CKV = C_CQ + MLA_Q_RANK
C_KPE = C_CKV + MLA_KV_RANK
C_DT = C_KPE + MLA_PAD
N_IN = C_DT + LANES

MLA_Q_SCALE = (MLA_NOPE + MLA_ROPE) ** -0.5 * 1.4426950408889634

SB_Q_SCALE = SB_HEAD_DIM ** -0.5 * 1.4426950408889634
SB_LOG2_FLOOR = -151.0


def _rms(x, g):
    return x * lax.rsqrt(jnp.mean(x * x, axis=-1, keepdims=True) + EPS) * g


def _silu(x):
    return x * (1.0 / (1.0 + jnp.exp(-x)))


def _softplus(x):
    return jnp.maximum(x, 0.0) + jnp.log1p(jnp.exp(-jnp.abs(x)))


def _split3(x):
    a = x.astype(BF16)
    r = x - a.astype(F32)
    b = r.astype(BF16)
    c = (r - b.astype(F32)).astype(BF16)
    return a, b, c


def _lane_tile(x, n):
    return jnp.concatenate([x] * n, axis=1)


def _dot(a, b):
    return jnp.dot(a, b, preferred_element_type=F32)


def _dot_nt(a, b):
    return lax.dot_general(a, b, (((1,), (1,)), ((), ())), preferred_element_type=F32)


def _dot_tn(a, b):
    return lax.dot_general(a, b, (((0,), (0,)), ((), ())), preferred_element_type=F32)


def _rope_table_kernel(pos_ref, invf_ref, sign_ref, cos_ref, sin_ref):
    ang = pos_ref[...].astype(F32) * invf_ref[...]
    cos_ref[...] = jnp.cos(ang)
    sin_ref[...] = jnp.sin(ang) * sign_ref[...]


def _rope_tables(pos_col, invf_lane, sign_lane, tm):
    t = pos_col.shape[0]
    return pl.pallas_call(
        _rope_table_kernel,
        grid=(t // tm,),
        in_specs=[pl.BlockSpec((tm, 1), lambda i: (i, 0)),
                  pl.BlockSpec((1, LANES), lambda i: (0, 0)),
                  pl.BlockSpec((1, LANES), lambda i: (0, 0))],
        out_specs=[pl.BlockSpec((tm, LANES), lambda i: (i, 0)),
                   pl.BlockSpec((tm, LANES), lambda i: (i, 0))],
        out_shape=[jax.ShapeDtypeStruct((t, LANES), F32)] * 2,
        compiler_params=pltpu.CompilerParams(dimension_semantics=("parallel",)),
        name="rope_tables",
    )(pos_col, invf_lane, sign_lane)


def _mix_in_kernel(h_ref, g_ref, w_ref, qn_ref, wuq_ref, kvn_ref, wk_ref, wv_ref, cos_ref, sin_ref,
                   sb_ref, z_ref, xbc_ref, dt_ref, qm_ref, km_ref, vm_ref):
    xn = _rms(h_ref[...], g_ref[...]).astype(BF16)

    def proj(lo, hi):
        return _dot(xn, w_ref[:, lo:hi])

    sb = proj(C_SB, C_Z)
    sb_ref[:, :SB_WIDTH] = (sb[:, :SB_WIDTH] * SB_Q_SCALE).astype(BF16)
    sb_ref[:, SB_WIDTH:] = sb[:, SB_WIDTH:].astype(BF16)
    z_ref[...] = proj(C_Z, C_XBC)
    xbc_ref[...] = proj(C_XBC, C_CQ)
    dt_ref[...] = proj(C_DT, N_IN)

    cos = cos_ref[...]
    sin = sin_ref[...]

    def rope(x):
        return x * cos + pltpu.roll(x, LANES // 2, axis=1) * sin

    cq = _rms(proj(C_CQ, C_CKV), qn_ref[...]).astype(BF16)
    qf = _dot(cq, wuq_ref[...])
    ckv = _rms(proj(C_CKV, C_KPE), kvn_ref[...]).astype(BF16)
    kn = _dot(ckv, wk_ref[...])
    lane = lax.broadcasted_iota(jnp.int32, (1, MLA_HEADS * LANES), 1)
    ones = ((lane % LANES < MLA_V) == ((lane // LANES) % 2 == 1)).astype(F32)
    vm_ref[...] = (_dot(ckv, wv_ref[...]) + ones).astype(BF16)
    kpe = rope(proj(C_KPE, C_DT))
    for h in range(MLA_HEADS):
        sl = slice(h * MLA_PAD, (h + 1) * MLA_PAD)
        qm_ref[:, sl] = (rope(qf[:, sl]) * MLA_Q_SCALE).astype(BF16)
        km_ref[:, sl] = (kn[:, sl] + kpe).astype(BF16)


def _mix_in(h, g, w_in_p, qn, wuq_p, kvn, wk_p, wv_p, cos_t, sin_t, tm):
    t = h.shape[0]
    row = lambda n: pl.BlockSpec((tm, n), lambda i: (i, 0))
    full = lambda a: pl.BlockSpec(a.shape, lambda i: (0, 0), pipeline_mode=pl.Buffered(1))
    out_widths = [(3 * SB_WIDTH, BF16), (SSM_INNER, F32), (SSM_CONV_DIM, F32), (LANES, F32),
                  (MLA_HEADS * MLA_PAD, BF16), (MLA_HEADS * MLA_PAD, BF16),
                  (MLA_HEADS * LANES, BF16)]
    return pl.pallas_call(
        _mix_in_kernel,
        grid=(t // tm,),
        in_specs=[row(D_MODEL), full(g), full(w_in_p), full(qn), full(wuq_p), full(kvn), full(wk_p),
                  full(wv_p), row(LANES), row(LANES)],
        out_specs=[row(n) for n, _ in out_widths],
        out_shape=[jax.ShapeDtypeStruct((t, n), d) for n, d in out_widths],
        compiler_params=pltpu.CompilerParams(dimension_semantics=("parallel",),
                                             vmem_limit_bytes=VMEM_LIMIT),
        name="mix_in",
    )(h, g, w_in_p, qn, wuq_p, kvn, wk_p, wv_p, cos_t, sin_t)


def _sb_kernel(q_ref, k_ref, v_ref, o_ref, c_ref, acc_ref, *, tq):
    tk = tq
    qi = pl.program_id(1)
    upper = (lax.broadcasted_iota(jnp.int32, (tk, tk), 0) >
             lax.broadcasted_iota(jnp.int32, (tk, tk), 1)).astype(BF16)
    lane = lax.broadcasted_iota(jnp.int32, (tq, LANES), 1)
    c_ref[...] = jnp.zeros(c_ref.shape, F32)
    acc_ref[...] = jnp.zeros(acc_ref.shape, F32)

    heads = range(SB_HEADS)
    pair = [slice((h // 2) * LANES, (h // 2 + 1) * LANES) for h in heads]

    def step(j, masked):
        ks = pl.multiple_of(j * tk, tk)
        z = []
        for h in heads:
            own = (lane < SB_HEAD_DIM) if h % 2 == 0 else (lane >= SB_HEAD_DIM)
            qh = jnp.where(own, q_ref[:, pair[h]], jnp.zeros((), BF16))
            z.append(_dot_nt(qh, k_ref[pl.ds(ks, tk), pair[h]]))
        log_keep = [-(jnp.maximum(x, 0.0) + jnp.log2(1.0 + jnp.exp2(-jnp.abs(x))))
                    for x in z]
        if masked:
            mask = (lax.broadcasted_iota(jnp.int32, (tq, tk), 1) <
                    lax.broadcasted_iota(jnp.int32, (tq, tk), 0))
            log_keep = [jnp.where(mask, x, 0.0) for x in log_keep]
        hi = [x.astype(BF16) for x in log_keep]
        lo = [(log_keep[h] - hi[h].astype(F32)).astype(BF16) for h in heads]
        tail = [_dot(hi[h], upper) + _dot(lo[h], upper) for h in heads]
        c_all = None
        pv = []
        for h in heads:
            c = c_ref[h]
            later = tail[h] + _lane_tile(c, tk // LANES)
            w = jnp.exp2(z[h] + log_keep[h] + later)
            if masked:
                w = jnp.where(mask, w, 0.0)
            pv.append(_dot(w.astype(BF16), v_ref[pl.ds(ks, tk), pair[h]]))
            c = c + jnp.sum(log_keep[h], axis=1, keepdims=True)
            c_ref[h] = c
            c_all = c if c_all is None else jnp.maximum(c_all, c)
        for h in heads:
            acc_ref[h] += pv[h]
        return jnp.max(c_all)

    def cond(carry):
        j, c_max = carry
        return jnp.logical_and(j >= 0, c_max > SB_LOG2_FLOOR)

    def body(carry):
        j, _ = carry
        return j - 1, step(j, False)

    lax.while_loop(cond, body, (qi - 1, step(qi, True)))
    for hp in range(SB_HEADS // 2):
        o_ref[:, hp * LANES:(hp + 1) * LANES] = jnp.where(lane < SB_HEAD_DIM, acc_ref[2 * hp],
                                                          acc_ref[2 * hp + 1])


def _sb_attn(sb, b, s, tq):
    nq = s // tq
    return pl.pallas_call(
        functools.partial(_sb_kernel, tq=tq),
        grid=(b, nq),
        in_specs=[pl.BlockSpec((tq, SB_WIDTH), lambda bi, qi: (bi * nq + qi, 0)),
                  pl.BlockSpec((s, SB_WIDTH), lambda bi, qi: (bi, 1)),
                  pl.BlockSpec((s, SB_WIDTH), lambda bi, qi: (bi, 2))],
        out_specs=pl.BlockSpec((tq, SB_WIDTH), lambda bi, qi: (bi * nq + qi, 0)),
        out_shape=jax.ShapeDtypeStruct((b * s, SB_WIDTH), F32),
        scratch_shapes=[pltpu.VMEM((SB_HEADS, tq, LANES), F32),
                        pltpu.VMEM((SB_HEADS, tq, LANES), F32)],
        compiler_params=pltpu.CompilerParams(dimension_semantics=("parallel", "parallel"),
                                             vmem_limit_bytes=VMEM_LIMIT),
        name="sb_attn",
    )(sb, sb, sb)


def _ssd_kernel(xbc_ref, z_ref, dt_ref, cw_ref, cb_ref, dtb_ref, alog_ref, dl_ref, gn_ref,
                y_ref, tail_ref, xs_ref, state_ref, *, nch):
    l = SSM_CHUNK
    rows = nch * l
    hd = SSM_HEAD_DIM
    pairs = SSM_HEADS // 2

    @pl.when(pl.program_id(1) == 0)
    def _():
        tail_ref[...] = jnp.zeros_like(tail_ref)
        state_ref[...] = jnp.zeros_like(state_ref)

    xs_ref[0:SUBLANES, :] = tail_ref[...]
    xs_ref[SUBLANES:SUBLANES + rows, :] = xbc_ref[...]
    tail_ref[...] = xbc_ref[rows - SUBLANES:rows, :]
    conv = cb_ref[...]
    for k in range(SSM_CONV):
        conv = conv + cw_ref[k:k + 1, :] * xs_ref[pl.ds(SUBLANES - (SSM_CONV - 1) + k, rows), :]
    xa = _silu(conv)
    dt_all = _softplus(dt_ref[...] + dtb_ref[...])
    da_all = dt_all * (-jnp.exp(alog_ref[...]))

    tril = (lax.broadcasted_iota(jnp.int32, (l, l), 0) >=
            lax.broadcasted_iota(jnp.int32, (l, l), 1))
    trilb = tril.astype(BF16)
    lo_half = lax.broadcasted_iota(jnp.int32, (l, LANES), 1) < hd
    lo_half_s = lax.broadcasted_iota(jnp.int32, (SSM_STATE, LANES), 1) < hd
    gn = SSM_GROUPS * SSM_STATE

    for c in range(nch):
        rs = slice(c * l, (c + 1) * l)
        x = xa[rs, :SSM_INNER]
        xb = x.astype(BF16)
        bm = xa[rs, SSM_INNER:SSM_INNER + gn]
        cm = xa[rs, SSM_INNER + gn:]
        d1, d2, d3 = _split3(da_all[rs])
        cs = _dot(trilb, d1) + _dot(trilb, d2) + _dot(trilb, d3)
        cs8 = cs.T[0:SSM_HEADS]
        dt8 = dt_all[rs].T[0:SSM_HEADS]
        bm_t = bm.T
        end8 = cs8[:, l - 1:l]
        w8 = jnp.exp(end8 - cs8) * dt8
        dec8 = jnp.broadcast_to(jnp.exp(end8), (SSM_HEADS, LANES))
        cg = [cm[:, g * SSM_STATE:(g + 1) * SSM_STATE].astype(BF16) for g in range(SSM_GROUPS)]
        bg_t = [bm_t[g * SSM_STATE:(g + 1) * SSM_STATE] for g in range(SSM_GROUPS)]
        cb = [_dot(cg[g], bg_t[g].astype(BF16)) for g in range(SSM_GROUPS)]

        ys = []
        for p in range(pairs):
            g = (2 * p) // (SSM_HEADS // SSM_GROUPS)
            ps = slice(p * LANES, (p + 1) * LANES)
            xp = xb[:, ps]
            y_diag, contrib, col = [], [], []
            for h in (2 * p, 2 * p + 1):
                cs_col = jnp.broadcast_to(cs[:, h:h + 1], (l, l))
                decay = jnp.exp(cs_col - cs8[h:h + 1, :])
                m = jnp.where(tril, cb[g] * decay * dt8[h:h + 1, :], 0.0)
                y_diag.append(_dot(m.astype(BF16), xp))
                contrib.append(_dot((bg_t[g] * w8[h:h + 1, :]).astype(BF16), xp))
                col.append(cs_col)
            st = state_ref[p]
            y_off = _dot(cg[g], st.astype(BF16)) * jnp.exp(jnp.where(lo_half, col[0], col[1]))
            st_decay = jnp.where(lo_half_s, dec8[2 * p:2 * p + 1, :], dec8[2 * p + 1:2 * p + 2, :])
            state_ref[p] = st * st_decay + jnp.where(lo_half_s, contrib[0], contrib[1])
            ys.append(jnp.where(lo_half, y_diag[0], y_diag[1]) + y_off + x[:, ps] * dl_ref[:, ps])
        y = jnp.concatenate(ys, axis=1) * _silu(z_ref[rs, :])
        y_ref[rs, :] = _rms(y, gn_ref[...]).astype(BF16)


def _ssd(xbc, z, dt, cw, cb, dtb, alog, dl, gn, b, s, nch):
    rows = nch * SSM_CHUNK
    nb = s // rows
    row = lambda n: pl.BlockSpec((rows, n), lambda bi, ci: (bi * nb + ci, 0))
    full = lambda a: pl.BlockSpec(a.shape, lambda bi, ci: (0, 0))
    return pl.pallas_call(
        functools.partial(_ssd_kernel, nch=nch),
        grid=(b, nb),
        in_specs=[row(SSM_CONV_DIM), row(SSM_INNER), row(LANES), full(cw), full(cb), full(dtb),
                  full(alog), full(dl), full(gn)],
        out_specs=row(SSM_INNER),
        out_shape=jax.ShapeDtypeStruct((b * s, SSM_INNER), BF16),
        scratch_shapes=[pltpu.VMEM((SUBLANES, SSM_CONV_DIM), F32),
                        pltpu.VMEM((SUBLANES + rows, SSM_CONV_DIM), F32),
                        pltpu.VMEM((SSM_HEADS // 2, SSM_STATE, LANES), F32)],
        compiler_params=pltpu.CompilerParams(dimension_semantics=("parallel", "arbitrary"),
                                             vmem_limit_bytes=VMEM_LIMIT),
        name="ssd",
    )(xbc, z, dt, cw, cb, dtb, alog, dl, gn)


def _mla_kernel(q_ref, k_ref, v_ref, o_ref, m_ref, acc_ref, *, tq, tk):
    qi = pl.program_id(1)
    diag_tiles = tq // tk
    m_ref[...] = jnp.full(m_ref.shape, -1e30, F32)
    acc_ref[...] = jnp.zeros(acc_ref.shape, F32)

    heads = range(MLA_HEADS)

    def step(j, diag):
        ks = pl.multiple_of(j * tk, tk)
        r0 = 0 if diag is None else diag * tk
        rs = slice(r0, tq)
        nr = tq - r0
        sc = []
        for h in heads:
            hs = slice(h * MLA_PAD, (h + 1) * MLA_PAD)
            sc.append(_dot_nt(q_ref[rs, hs], k_ref[pl.ds(ks, tk), hs]))
        if diag is not None:
            keep = (lax.broadcasted_iota(jnp.int32, (nr, tk), 1) <=
                    lax.broadcasted_iota(jnp.int32, (nr, tk), 0))
            sc = [jnp.where(keep, x, -jnp.inf) for x in sc]
        alpha, pv = [], []
        for h in heads:
            m_old = m_ref[h, rs, :]
            m_new = jnp.maximum(m_old, jnp.max(sc[h], axis=1, keepdims=True))
            p = jnp.exp2(sc[h] - _lane_tile(m_new, tk // LANES))
            m_ref[h, rs, :] = m_new
            alpha.append(jnp.exp2(m_old - m_new))
            pv.append(_dot(p.astype(BF16), v_ref[pl.ds(ks, tk), h * LANES:(h + 1) * LANES]))
        for h in heads:
            acc_ref[h, rs, :] = alpha[h] * acc_ref[h, rs, :] + pv[h]

    def body(j, carry):
        step(j, None)
        return carry

    lax.fori_loop(0, qi * diag_tiles, body, 0)
    for u in range(diag_tiles):
        step(qi * diag_tiles + u, u)
    lane = lax.broadcasted_iota(jnp.int32, (tq, LANES), 1)
    for hp in range(MLA_HEADS // 2):
        even = acc_ref[2 * hp]
        odd = acc_ref[2 * hp + 1]
        even = even / pltpu.roll(even, LANES // 2, axis=1)
        odd = odd / pltpu.roll(odd, LANES // 2, axis=1)
        o_ref[:, hp * LANES:(hp + 1) * LANES] = jnp.where(lane < MLA_V, even, odd)


def _mla_attn(qm, km, vm, b, s, tq, tk):
    nq = s // tq
    return pl.pallas_call(
        functools.partial(_mla_kernel, tq=tq, tk=tk),
        grid=(b, nq),
        in_specs=[pl.BlockSpec((tq, MLA_HEADS * MLA_PAD), lambda bi, qi: (bi * nq + qi, 0)),
                  pl.BlockSpec((s, MLA_HEADS * MLA_PAD), lambda bi, qi: (bi, 0),
                               pipeline_mode=pl.Buffered(1)),
                  pl.BlockSpec((s, MLA_HEADS * LANES), lambda bi, qi: (bi, 0),
                               pipeline_mode=pl.Buffered(1))],
        out_specs=pl.BlockSpec((tq, MLA_WIDTH), lambda bi, qi: (bi * nq + qi, 0)),
        out_shape=jax.ShapeDtypeStruct((b * s, MLA_WIDTH), F32),
        scratch_shapes=[pltpu.VMEM((MLA_HEADS, tq, LANES), F32),
                        pltpu.VMEM((MLA_HEADS, tq, LANES), F32)],
        compiler_params=pltpu.CompilerParams(dimension_semantics=("parallel", "parallel"),
                                             vmem_limit_bytes=VMEM_LIMIT),
        name="mla_attn",
    )(qm, km, vm)


def _out_ffn_kernel(h_ref, ya_ref, yb_ref, yc_ref, ga_ref, gc_ref, wo_ref, g_ref, wg_ref, wv_ref,
                    cwg_ref, cwv_ref, cbg_ref, cbv_ref, wd_ref, fin_ref,
                    o_ref, hn_ref, acc_ref, carry_g, carry_v, sg0, sg1, sv0, sv1, act0, act1, *, tm,
                    tiles_per_seq, final_norm):
    ya = _rms(ya_ref[...], ga_ref[...]).astype(BF16)
    yc = _rms(yc_ref[...], gc_ref[...]).astype(BF16)
    c1 = SB_WIDTH
    c2 = SB_WIDTH + SSM_INNER
    h1 = (h_ref[...] + _dot(ya, wo_ref[:c1, :]) + _dot(yb_ref[...], wo_ref[c1:c2, :])
          + _dot(yc, wo_ref[c2:, :]))
    hn_ref[...] = _rms(h1, g_ref[...]).astype(BF16)
    acc_ref[...] = h1

    first = (pl.program_id(0) % tiles_per_seq) == 0
    pad = SUBLANES
    nj = wg_ref.shape[0]

    sg, sv, act = (sg0, sg1), (sv0, sv1), (act0, act1)

    def up_proj(j, slot):
        for w_ref, carry_ref, s_ref in ((wg_ref, carry_g, sg[slot]), (wv_ref, carry_v, sv[slot])):
            up = _dot(hn_ref[...], w_ref[j])
            s_ref[0:pad, :] = jnp.where(first, 0.0, carry_ref[j])
            s_ref[pad:pad + tm, :] = up
            carry_ref[j] = up[tm - pad:tm, :]

    def conv_act(j, slot):
        def conv(cw_ref, cb_ref, s_ref):
            cw = cw_ref[j]
            return (cb_ref[j] + cw[0:1, :] * s_ref[pl.ds(pad - 2, tm), :]
                    + cw[1:2, :] * s_ref[pl.ds(pad - 1, tm), :]
                    + cw[2:3, :] * s_ref[pl.ds(pad, tm), :])

        act[slot][...] = (_silu(conv(cwg_ref, cbg_ref, sg[slot]))
                          * conv(cwv_ref, cbv_ref, sv[slot])).astype(BF16)

    def down_proj(j, slot):
        acc_ref[...] += _dot(act[slot][...], wd_ref[j])

    def steady(j, slot):
        up_proj(j + 1, 1 - slot)
        conv_act(j, slot)
        down_proj(j, slot)

    assert nj % 2 == 1
    up_proj(0, 0)

    def pair(p, carry):
        steady(2 * p, 0)
        steady(2 * p + 1, 1)
        return carry

    lax.fori_loop(0, (nj - 1) // 2, pair, 0)
    conv_act(nj - 1, 0)
    down_proj(nj - 1, 0)
    out = acc_ref[...]
    if final_norm:
        out = _rms(out, fin_ref[...])
    o_ref[...] = out


def _out_ffn(h, ya, yb, yc, ga, gc, w_out, g, wg, wv, cwg, cwv, cbg, cbv, wd, fin, s, tm, final_norm):
    t = h.shape[0]
    nj, _, tf = wg.shape
    row = lambda n: pl.BlockSpec((tm, n), lambda i: (i, 0))

    def resident(a):
        return pl.BlockSpec(a.shape, lambda i: (0,) * a.ndim, pipeline_mode=pl.Buffered(1))

    return pl.pallas_call(
        functools.partial(_out_ffn_kernel, tm=tm, tiles_per_seq=s // tm, final_norm=final_norm),
        grid=(t // tm,),
        in_specs=[row(D_MODEL), row(SB_WIDTH), row(SSM_INNER), row(MLA_WIDTH)]
                 + [resident(a) for a in (ga, gc, w_out, g, wg, wv, cwg, cwv, cbg, cbv, wd, fin)],
        out_specs=row(D_MODEL),
        out_shape=jax.ShapeDtypeStruct((t, D_MODEL), F32),
        scratch_shapes=[pltpu.VMEM((tm, D_MODEL), BF16), pltpu.VMEM((tm, D_MODEL), F32),
                        pltpu.VMEM((nj, SUBLANES, tf), F32), pltpu.VMEM((nj, SUBLANES, tf), F32),
                        ] + [pltpu.VMEM((SUBLANES + tm, tf), F32)] * 4 + [pltpu.VMEM((tm, tf), BF16)] * 2,
        compiler_params=pltpu.CompilerParams(
            dimension_semantics=("arbitrary",), vmem_limit_bytes=VMEM_LIMIT_FFN),
        name="out_ffn",
    )(h, ya, yb, yc, ga, gc, w_out, g, wg, wv, cwg, cwv, cbg, cbv, wd, fin)


def _chunk_cols(a, tf):
    r, n = a.shape
    return a.reshape(r, n // tf, tf).transpose(1, 0, 2)


def _pack_in_proj(w):
    d = w.shape[0]
    o_dt = C_CQ
    o_cq = o_dt + SSM_HEADS
    o_kr = o_cq + MLA_Q_RANK + MLA_KV_RANK
    kr = w[:, o_kr:o_kr + MLA_ROPE]
    zc = lambda n: jnp.zeros((d, n), w.dtype)
    kpe = jnp.concatenate([zc(32), kr[:, :HALF], zc(16), zc(32), kr[:, HALF:], zc(16)], axis=1)
    dtb = jnp.concatenate([w[:, o_dt:o_cq], zc(LANES - SSM_HEADS)], axis=1)
    return jnp.concatenate([w[:, :o_dt], w[:, o_cq:o_kr], kpe, dtb], axis=1).astype(BF16)


def _pack_uq(w):
    r = w.shape[0]
    zc = lambda n: jnp.zeros((r, n), w.dtype)
    cols = []
    for h in range(MLA_HEADS):
        o = h * (MLA_NOPE + MLA_ROPE)
        nope = w[:, o:o + MLA_NOPE]
        rp = w[:, o + MLA_NOPE:o + MLA_NOPE + MLA_ROPE]
        cols += [nope[:, :32], rp[:, :HALF], zc(16), nope[:, 32:], rp[:, HALF:], zc(16)]
    return jnp.concatenate(cols, axis=1).astype(BF16)


def _pack_ukv(w):
    r = w.shape[0]
    zc = lambda n: jnp.zeros((r, n), w.dtype)
    kc, vc = [], []
    for h in range(MLA_HEADS):
        o = h * (MLA_NOPE + MLA_V)
        nope = w[:, o:o + MLA_NOPE]
        kc += [nope[:, :32], zc(32), nope[:, 32:], zc(32)]
        wv = w[:, o + MLA_NOPE:o + MLA_NOPE + MLA_V]
        vc += [wv, zc(LANES - MLA_V)] if h % 2 == 0 else [zc(LANES - MLA_V), wv]
    return jnp.concatenate(kc, axis=1).astype(BF16), jnp.concatenate(vc, axis=1).astype(BF16)


def _lane_pad(v):
    return jnp.concatenate([v, jnp.zeros((LANES - v.shape[0],), v.dtype)])[None, :]


TM_IN = 1024
TQ_SB = 256
TQ_MLA = 1024
TK_MLA = 512
SSD_BLOCK_CHUNKS = 4
TM_FFN = 1024
TF_FFN = 256


def kernel(x, positions, mix_norm, w_in, sb_out_norm, ssm_conv_w, ssm_conv_b, ssm_dt_bias, ssm_a_log, ssm_d, ssm_out_norm, mla_q_norm, mla_w_uq, mla_kv_norm, mla_w_ukv, mla_out_norm, w_out, ffn_norm, ffn_w_up, ffn_conv_w, ffn_conv_b, ffn_w_down, final_norm):
    b, s, d = x.shape
    t = b * s
    depth = w_in.shape[0]
    h = x.reshape(t, d)

    inv_freq = 1.0 / (ROPE_THETA ** (jnp.arange(0, MLA_ROPE, 2, dtype=F32) / MLA_ROPE))
    z16, z32 = jnp.zeros((16,), F32), jnp.zeros((32,), F32)
    invf_lane = jnp.concatenate([z32, inv_freq, z16, z32, inv_freq, z16])[None, :]
    sign_lane = jnp.concatenate([z32, -jnp.ones((16,), F32), z16, z32, jnp.ones((16,), F32), z16])[None, :]
    cos_t, sin_t = _rope_tables(positions.reshape(t, 1), invf_lane, sign_lane, TM_IN)

    for i in range(depth):
        wk_p, wv_p = _pack_ukv(mla_w_ukv[i])
        sb, z, xbc, dt, qm, km, vm = _mix_in(
            h, mix_norm[i][None, :], _pack_in_proj(w_in[i]), mla_q_norm[i][None, :],
            _pack_uq(mla_w_uq[i]), mla_kv_norm[i][None, :], wk_p, wv_p, cos_t, sin_t, TM_IN)
        ya = _sb_attn(sb, b, s, TQ_SB)
        yb = _ssd(xbc, z, dt, ssm_conv_w[i], ssm_conv_b[i][None, :], _lane_pad(ssm_dt_bias[i]),
                  _lane_pad(ssm_a_log[i]), jnp.repeat(ssm_d[i], SSM_HEAD_DIM)[None, :],
                  ssm_out_norm[i][None, :], b, s, SSD_BLOCK_CHUNKS)
        yc = _mla_attn(qm, km, vm, b, s, TQ_MLA, TK_MLA)
        w_up = ffn_w_up[i].astype(BF16)
        cw, cb = ffn_conv_w[i], ffn_conv_b[i][None, :]
        h = _out_ffn(h, ya, yb, yc, sb_out_norm[i][None, :], mla_out_norm[i][None, :],
                     w_out[i].astype(BF16), ffn_norm[i][None, :],
                     _chunk_cols(w_up[:, :D_FF], TF_FFN), _chunk_cols(w_up[:, D_FF:], TF_FFN),
                     _chunk_cols(cw[:, :D_FF], TF_FFN), _chunk_cols(cw[:, D_FF:], TF_FFN),
                     _chunk_cols(cb[:, :D_FF], TF_FFN), _chunk_cols(cb[:, D_FF:], TF_FFN),
                     ffn_w_down[i].astype(BF16).reshape(D_FF // TF_FFN, TF_FFN, D_MODEL),
                     final_norm[None, :], s, TM_FFN, final_norm=(i == depth - 1))
    return h.reshape(b, s, d)
```

```python
import functools

import jax
import jax.numpy as jnp
from jax import lax
from jax.experimental import pallas as pl
from jax.experimental.pallas import tpu as pltpu

F32 = jnp.float32
BF16 = jnp.bfloat16

D_MODEL = 1024
EPS = 1e-6
SB_HEADS = 4
SB_HEAD_DIM = 64
SB_WIDTH = SB_HEADS * SB_HEAD_DIM
SSM_HEADS = 8
SSM_HEAD_DIM = 64
SSM_INNER = SSM_HEADS * SSM_HEAD_DIM
SSM_GROUPS = 2
SSM_STATE = 64
SSM_CONV = 4
SSM_CHUNK = 128
SSM_CONV_DIM = SSM_INNER + 2 * SSM_GROUPS * SSM_STATE
MLA_HEADS = 4
MLA_NOPE = 64
MLA_ROPE = 32
MLA_V = 64
MLA_Q_RANK = 256
MLA_KV_RANK = 128
MLA_WIDTH = MLA_HEADS * MLA_V
ROPE_THETA = 10000.0
D_MIX = SB_WIDTH + SSM_INNER + MLA_WIDTH
D_FF = 2816
FFN_CONV = 3

LANES = 128
SUBLANES = 8
MLA_PAD = 128
HALF = MLA_ROPE // 2
VMEM_LIMIT = 48 * 1024 * 1024
VMEM_LIMIT_FFN = 62 * 1024 * 1024

C_SB = 0
C_Z = C_SB + 3 * SB_WIDTH
C_XBC = C_Z + SSM_INNER
C_CQ = C_XBC + SSM_CONV_DIM
C_CKV = C_CQ + MLA_Q_RANK
C_KPE = C_CKV + MLA_KV_RANK
C_DT = C_KPE + MLA_PAD
N_IN = C_DT + LANES

MLA_Q_SCALE = (MLA_NOPE + MLA_ROPE) ** -0.5 * 1.4426950408889634

SB_Q_SCALE = SB_HEAD_DIM ** -0.5 * 1.4426950408889634
SB_LOG2_FLOOR = -151.0


def _rms(x, g):
    return x * lax.rsqrt(jnp.mean(x * x, axis=-1, keepdims=True) + EPS) * g


def _silu(x):
    return x * (1.0 / (1.0 + jnp.exp(-x)))


def _softplus(x):
    return jnp.maximum(x, 0.0) + jnp.log1p(jnp.exp(-jnp.abs(x)))


def _split3(x):
    a = x.astype(BF16)
    r = x - a.astype(F32)
    b = r.astype(BF16)
    c = (r - b.astype(F32)).astype(BF16)
    return a, b, c


def _lane_tile(x, n):
    return jnp.concatenate([x] * n, axis=1)


def _dot(a, b):
    return jnp.dot(a, b, preferred_element_type=F32)


def _dot_nt(a, b):
    return lax.dot_general(a, b, (((1,), (1,)), ((), ())), preferred_element_type=F32)


def _layer_spec(a, layer):
    rest = a.shape[1:]
    return pl.BlockSpec((None,) + rest, lambda *_: (layer,) + (0,) * len(rest),
                        pipeline_mode=pl.Buffered(1))


def _rope_table_kernel(pos_ref, invf_ref, cos_ref, sin_ref, nsin_ref):
    ang = pos_ref[...].astype(F32) * invf_ref[...]
    sin = jnp.sin(ang)
    cos_ref[...] = jnp.cos(ang)
    sin_ref[...] = sin
    nsin_ref[...] = -sin


def _rope_tables(pos_dense, invf_dense, tm):
    r = pos_dense.shape[0]
    spec = pl.BlockSpec((tm, LANES), lambda i: (i, 0))
    return pl.pallas_call(
        _rope_table_kernel,
        grid=(r // tm,),
        in_specs=[spec, pl.BlockSpec((1, LANES), lambda i: (0, 0))],
        out_specs=[spec] * 3,
        out_shape=[jax.ShapeDtypeStruct((r, LANES), F32)] * 3,
        compiler_params=pltpu.CompilerParams(dimension_semantics=("parallel",)),
        name="rope_tables",
    )(pos_dense, invf_dense)


def _mix_in_kernel(h_ref, g_ref, w_ref, wt_ref, qn_ref, wuq_ref, kvn_ref, wk_ref, wv_ref, cos_ref,
                   sin_ref, sb_ref, z_ref, xbc_ref, dt_ref, qm_ref, km_ref, vm_ref):
    xn = _rms(h_ref[...], g_ref[...]).astype(BF16)

    def proj(lo, hi):
        if lo >= C_CQ:
            return _dot(xn, wt_ref[:, lo - C_CQ:hi - C_CQ])
        return _dot(xn, w_ref[:, lo:hi])

    sb = proj(C_SB, C_Z)
    sb_ref[:, :SB_WIDTH] = (sb[:, :SB_WIDTH] * SB_Q_SCALE).astype(BF16)
    sb_ref[:, SB_WIDTH:] = sb[:, SB_WIDTH:].astype(BF16)
    z_ref[...] = proj(C_Z, C_XBC)
    xbc_ref[...] = proj(C_XBC, C_CQ)
    dt_ref[...] = proj(C_DT, N_IN)

    cos = cos_ref[...]
    sin = sin_ref[...]

    def rope(x):
        return x * cos + pltpu.roll(x, LANES // 2, axis=1) * sin

    cq = _rms(proj(C_CQ, C_CKV), qn_ref[...]).astype(BF16)
    qf = _dot(cq, wuq_ref[...])
    ckv = _rms(proj(C_CKV, C_KPE), kvn_ref[...]).astype(BF16)
    kn = _dot(ckv, wk_ref[...])
    lane = lax.broadcasted_iota(jnp.int32, (1, MLA_HEADS * LANES), 1)
    ones = ((lane % LANES < MLA_V) == ((lane // LANES) % 2 == 1)).astype(F32)
    vm_ref[...] = (_dot(ckv, wv_ref[...]) + ones).astype(BF16)
    kpe = rope(proj(C_KPE, C_DT))
    for h in range(MLA_HEADS):
        sl = slice(h * MLA_PAD, (h + 1) * MLA_PAD)
        qm_ref[:, sl] = (rope(qf[:, sl]) * MLA_Q_SCALE).astype(BF16)
        km_ref[:, sl] = (kn[:, sl] + kpe).astype(BF16)


def _mix_in(h, params, cos_t, sin_t, layer, tm):
    t = h.shape[0]
    row = lambda n: pl.BlockSpec((tm, n), lambda i: (i, 0))
    out_widths = [(3 * SB_WIDTH, BF16), (SSM_INNER, F32), (SSM_CONV_DIM, F32), (LANES, F32),
                  (MLA_HEADS * MLA_PAD, BF16), (MLA_HEADS * MLA_PAD, BF16),
                  (MLA_HEADS * LANES, BF16)]
    return pl.pallas_call(
        _mix_in_kernel,
        grid=(t // tm,),
        in_specs=[row(D_MODEL)] + [_layer_spec(a, layer) for a in params] + [row(LANES), row(LANES)],
        out_specs=[row(n) for n, _ in out_widths],
        out_shape=[jax.ShapeDtypeStruct((t, n), d) for n, d in out_widths],
        compiler_params=pltpu.CompilerParams(dimension_semantics=("parallel",),
                                             vmem_limit_bytes=VMEM_LIMIT),
        name="mix_in",
    )(h, *params, cos_t, sin_t)


def _sb_kernel(q_ref, k_ref, v_ref, o_ref, c_ref, acc_ref, *, tq):
    tk = tq
    qi = pl.program_id(1)
    upper = (lax.broadcasted_iota(jnp.int32, (tk, tk), 0) >
             lax.broadcasted_iota(jnp.int32, (tk, tk), 1)).astype(BF16)
    lane = lax.broadcasted_iota(jnp.int32, (tq, LANES), 1)
    c_ref[...] = jnp.zeros(c_ref.shape, F32)
    acc_ref[...] = jnp.zeros(acc_ref.shape, F32)

    heads = range(SB_HEADS)
    pair = [slice((h // 2) * LANES, (h // 2 + 1) * LANES) for h in heads]

    def step(j, masked):
        ks = pl.multiple_of(j * tk, tk)
        z = []
        for h in heads:
            own = (lane < SB_HEAD_DIM) if h % 2 == 0 else (lane >= SB_HEAD_DIM)
            qh = jnp.where(own, q_ref[:, pair[h]], jnp.zeros((), BF16))
            z.append(_dot_nt(qh, k_ref[pl.ds(ks, tk), pair[h]]))
        log_keep = [-(jnp.maximum(x, 0.0) + jnp.log2(1.0 + jnp.exp2(-jnp.abs(x))))
                    for x in z]
        if masked:
            mask = (lax.broadcasted_iota(jnp.int32, (tq, tk), 1) <
                    lax.broadcasted_iota(jnp.int32, (tq, tk), 0))
            log_keep = [jnp.where(mask, x, 0.0) for x in log_keep]
        hi = [x.astype(BF16) for x in log_keep]
        lo = [(log_keep[h] - hi[h].astype(F32)).astype(BF16) for h in heads]
        tail = [_dot(hi[h], upper) + _dot(lo[h], upper) for h in heads]
        c_all = None
        pv = []
        for h in heads:
            c = c_ref[h]
            later = tail[h] + _lane_tile(c, tk // LANES)
            w = jnp.exp2(z[h] + log_keep[h] + later)
            if masked:
                w = jnp.where(mask, w, 0.0)
            pv.append(_dot(w.astype(BF16), v_ref[pl.ds(ks, tk), pair[h]]))
            c = c + jnp.sum(log_keep[h], axis=1, keepdims=True)
            c_ref[h] = c
            c_all = c if c_all is None else jnp.maximum(c_all, c)
        for h in heads:
            acc_ref[h] += pv[h]
        return jnp.max(c_all)

    def cond(carry):
        j, c_max = carry
        return jnp.logical_and(j >= 0, c_max > SB_LOG2_FLOOR)

    def body(carry):
        j, _ = carry
        return j - 1, step(j, False)

    lax.while_loop(cond, body, (qi - 1, step(qi, True)))
    for hp in range(SB_HEADS // 2):
        o_ref[:, hp * LANES:(hp + 1) * LANES] = jnp.where(lane < SB_HEAD_DIM, acc_ref[2 * hp],
                                                          acc_ref[2 * hp + 1])


def _sb_attn(sb, b, s, tq):
    nq = s // tq
    return pl.pallas_call(
        functools.partial(_sb_kernel, tq=tq),
        grid=(b, nq),
        in_specs=[pl.BlockSpec((tq, SB_WIDTH), lambda bi, qi: (bi * nq + qi, 0)),
                  pl.BlockSpec((s, SB_WIDTH), lambda bi, qi: (bi, 1)),
                  pl.BlockSpec((s, SB_WIDTH), lambda bi, qi: (bi, 2))],
        out_specs=pl.BlockSpec((tq, SB_WIDTH), lambda bi, qi: (bi * nq + qi, 0)),
        out_shape=jax.ShapeDtypeStruct((b * s, SB_WIDTH), F32),
        scratch_shapes=[pltpu.VMEM((SB_HEADS, tq, LANES), F32),
                        pltpu.VMEM((SB_HEADS, tq, LANES), F32)],
        compiler_params=pltpu.CompilerParams(dimension_semantics=("parallel", "parallel"),
                                             vmem_limit_bytes=VMEM_LIMIT),
        name="sb_attn",
    )(sb, sb, sb)


def _ssd_kernel(xbc_ref, z_ref, dt_ref, cw_ref, cb_ref, dtb_ref, alog_ref, dl_ref, gn_ref,
                y_ref, tail_ref, xs_ref, state_ref, *, nch):
    l = SSM_CHUNK
    rows = nch * l
    hd = SSM_HEAD_DIM
    pairs = SSM_HEADS // 2

    @pl.when(pl.program_id(1) == 0)
    def _():
        tail_ref[...] = jnp.zeros_like(tail_ref)
        state_ref[...] = jnp.zeros_like(state_ref)

    xs_ref[0:SUBLANES, :] = tail_ref[...]
    xs_ref[SUBLANES:SUBLANES + rows, :] = xbc_ref[...]
    tail_ref[...] = xbc_ref[rows - SUBLANES:rows, :]
    conv = cb_ref[...]
    for k in range(SSM_CONV):
        conv = conv + cw_ref[k:k + 1, :] * xs_ref[pl.ds(SUBLANES - (SSM_CONV - 1) + k, rows), :]
    xa = _silu(conv)
    dt_all = _softplus(dt_ref[...] + dtb_ref[...])
    da_all = dt_all * (-jnp.exp(alog_ref[...]))

    tril = (lax.broadcasted_iota(jnp.int32, (l, l), 0) >=
            lax.broadcasted_iota(jnp.int32, (l, l), 1))
    trilb = tril.astype(BF16)
    lo_half = lax.broadcasted_iota(jnp.int32, (l, LANES), 1) < hd
    lo_half_s = lax.broadcasted_iota(jnp.int32, (SSM_STATE, LANES), 1) < hd
    gn = SSM_GROUPS * SSM_STATE

    for c in range(nch):
        rs = slice(c * l, (c + 1) * l)
        x = xa[rs, :SSM_INNER]
        xb = x.astype(BF16)
        bm = xa[rs, SSM_INNER:SSM_INNER + gn]
        cm = xa[rs, SSM_INNER + gn:]
        d1, d2, d3 = _split3(da_all[rs])
        cs = _dot(trilb, d1) + _dot(trilb, d2) + _dot(trilb, d3)
        cs8 = cs.T[0:SSM_HEADS]
        dt8 = dt_all[rs].T[0:SSM_HEADS]
        bm_t = bm.T
        end8 = cs8[:, l - 1:l]
        w8 = jnp.exp(end8 - cs8) * dt8
        dec8 = jnp.broadcast_to(jnp.exp(end8), (SSM_HEADS, LANES))
        cg = [cm[:, g * SSM_STATE:(g + 1) * SSM_STATE].astype(BF16) for g in range(SSM_GROUPS)]
        bg_t = [bm_t[g * SSM_STATE:(g + 1) * SSM_STATE] for g in range(SSM_GROUPS)]
        cb = [_dot(cg[g], bg_t[g].astype(BF16)) for g in range(SSM_GROUPS)]

        ys = []
        for p in range(pairs):
            g = (2 * p) // (SSM_HEADS // SSM_GROUPS)
            ps = slice(p * LANES, (p + 1) * LANES)
            xp = xb[:, ps]
            y_diag, contrib, col = [], [], []
            for h in (2 * p, 2 * p + 1):
                cs_col = jnp.broadcast_to(cs[:, h:h + 1], (l, l))
                decay = jnp.exp(cs_col - cs8[h:h + 1, :])
                m = jnp.where(tril, cb[g] * decay * dt8[h:h + 1, :], 0.0)
                y_diag.append(_dot(m.astype(BF16), xp))
                contrib.append(_dot((bg_t[g] * w8[h:h + 1, :]).astype(BF16), xp))
                col.append(cs_col)
            st = state_ref[p]
            y_off = _dot(cg[g], st.astype(BF16)) * jnp.exp(jnp.where(lo_half, col[0], col[1]))
            st_decay = jnp.where(lo_half_s, dec8[2 * p:2 * p + 1, :], dec8[2 * p + 1:2 * p + 2, :])
            state_ref[p] = st * st_decay + jnp.where(lo_half_s, contrib[0], contrib[1])
            ys.append(jnp.where(lo_half, y_diag[0], y_diag[1]) + y_off + x[:, ps] * dl_ref[:, ps])
        y = jnp.concatenate(ys, axis=1) * _silu(z_ref[rs, :])
        y_ref[rs, :] = _rms(y, gn_ref[...]).astype(BF16)


def _ssd(xbc, z, dt, params, layer, b, s, nch):
    rows = nch * SSM_CHUNK
    nb = s // rows
    row = lambda n: pl.BlockSpec((rows, n), lambda bi, ci: (bi * nb + ci, 0))
    return pl.pallas_call(
        functools.partial(_ssd_kernel, nch=nch),
        grid=(b, nb),
        in_specs=[row(SSM_CONV_DIM), row(SSM_INNER), row(LANES)]
                 + [_layer_spec(a, layer) for a in params],
        out_specs=row(SSM_INNER),
        out_shape=jax.ShapeDtypeStruct((b * s, SSM_INNER), BF16),
        scratch_shapes=[pltpu.VMEM((SUBLANES, SSM_CONV_DIM), F32),
                        pltpu.VMEM((SUBLANES + rows, SSM_CONV_DIM), F32),
                        pltpu.VMEM((SSM_HEADS // 2, SSM_STATE, LANES), F32)],
        compiler_params=pltpu.CompilerParams(dimension_semantics=("parallel", "arbitrary"),
                                             vmem_limit_bytes=VMEM_LIMIT),
        name="ssd",
    )(xbc, z, dt, *params)


def _mla_kernel(q_ref, k_ref, v_ref, o_ref, m_ref, acc_ref, *, tq, tk):
    qi = pl.program_id(1)
    diag_tiles = tq // tk
    m_ref[...] = jnp.full(m_ref.shape, -1e30, F32)
    acc_ref[...] = jnp.zeros(acc_ref.shape, F32)

    heads = range(MLA_HEADS)

    def step(j, diag):
        ks = pl.multiple_of(j * tk, tk)
        r0 = 0 if diag is None else diag * tk
        rs = slice(r0, tq)
        nr = tq - r0
        sc = []
        for h in heads:
            hs = slice(h * MLA_PAD, (h + 1) * MLA_PAD)
            sc.append(_dot_nt(q_ref[rs, hs], k_ref[pl.ds(ks, tk), hs]))
        if diag is not None:
            keep = (lax.broadcasted_iota(jnp.int32, (nr, tk), 1) <=
                    lax.broadcasted_iota(jnp.int32, (nr, tk), 0))
            sc = [jnp.where(keep, x, -jnp.inf) for x in sc]
        alpha, pv = [], []
        for h in heads:
            m_old = m_ref[h, rs, :]
            m_new = jnp.maximum(m_old, jnp.max(sc[h], axis=1, keepdims=True))
            p = jnp.exp2(sc[h] - _lane_tile(m_new, tk // LANES))
            m_ref[h, rs, :] = m_new
            alpha.append(jnp.exp2(m_old - m_new))
            pv.append(_dot(p.astype(BF16), v_ref[pl.ds(ks, tk), h * LANES:(h + 1) * LANES]))
        for h in heads:
            acc_ref[h, rs, :] = alpha[h] * acc_ref[h, rs, :] + pv[h]

    def body(j, carry):
        step(j, None)
        return carry

    lax.fori_loop(0, qi * diag_tiles, body, 0)
    for u in range(diag_tiles):
        step(qi * diag_tiles + u, u)
    lane = lax.broadcasted_iota(jnp.int32, (tq, LANES), 1)
    for hp in range(MLA_HEADS // 2):
        even = acc_ref[2 * hp]
        odd = acc_ref[2 * hp + 1]
        even = even / pltpu.roll(even, LANES // 2, axis=1)
        odd = odd / pltpu.roll(odd, LANES // 2, axis=1)
        o_ref[:, hp * LANES:(hp + 1) * LANES] = jnp.where(lane < MLA_V, even, odd)


def _mla_attn(qm, km, vm, b, s, tq, tk):
    nq = s // tq
    return pl.pallas_call(
        functools.partial(_mla_kernel, tq=tq, tk=tk),
        grid=(b, nq),
        in_specs=[pl.BlockSpec((tq, MLA_HEADS * MLA_PAD), lambda bi, qi: (bi * nq + qi, 0)),
                  pl.BlockSpec((s, MLA_HEADS * MLA_PAD), lambda bi, qi: (bi, 0),
                               pipeline_mode=pl.Buffered(1)),
                  pl.BlockSpec((s, MLA_HEADS * LANES), lambda bi, qi: (bi, 0),
                               pipeline_mode=pl.Buffered(1))],
        out_specs=pl.BlockSpec((tq, MLA_WIDTH), lambda bi, qi: (bi * nq + qi, 0)),
        out_shape=jax.ShapeDtypeStruct((b * s, MLA_WIDTH), F32),
        scratch_shapes=[pltpu.VMEM((MLA_HEADS, tq, LANES), F32),
                        pltpu.VMEM((MLA_HEADS, tq, LANES), F32)],
        compiler_params=pltpu.CompilerParams(dimension_semantics=("parallel", "parallel"),
                                             vmem_limit_bytes=VMEM_LIMIT),
        name="mla_attn",
    )(qm, km, vm)


def _out_ffn_kernel(h_ref, ya_ref, yb_ref, yc_ref, ga_ref, gc_ref, wo_ref, g_ref, wup_ref, cw_ref,
                    cb_ref, wd_ref, fin_ref,
                    o_ref, hn_ref, acc_ref, carry_g, carry_v, sg0, sg1, sv0, sv1, act0, act1, *, tm,
                    tiles_per_seq, final_norm):
    ya = _rms(ya_ref[...], ga_ref[...]).astype(BF16)
    yc = _rms(yc_ref[...], gc_ref[...]).astype(BF16)
    c1 = SB_WIDTH
    c2 = SB_WIDTH + SSM_INNER
    h1 = (h_ref[...] + _dot(ya, wo_ref[:c1, :]) + _dot(yb_ref[...], wo_ref[c1:c2, :])
          + _dot(yc, wo_ref[c2:, :]))
    hn_ref[...] = _rms(h1, g_ref[...]).astype(BF16)
    acc_ref[...] = h1

    first = (pl.program_id(0) % tiles_per_seq) == 0
    pad = SUBLANES
    nj = wup_ref.shape[1]

    sg, sv, act = (sg0, sg1), (sv0, sv1), (act0, act1)

    def up_proj(j, slot):
        for half, carry_ref, s_ref in ((0, carry_g, sg[slot]), (1, carry_v, sv[slot])):
            up = _dot(hn_ref[...], wup_ref[half, j])
            s_ref[0:pad, :] = jnp.where(first, 0.0, carry_ref[j])
            s_ref[pad:pad + tm, :] = up
            carry_ref[j] = up[tm - pad:tm, :]

    def conv_act(j, slot):
        def conv(half, s_ref):
            cw = cw_ref[half, j]
            return (cb_ref[half, j] + cw[0:1, :] * s_ref[pl.ds(pad - 2, tm), :]
                    + cw[1:2, :] * s_ref[pl.ds(pad - 1, tm), :]
                    + cw[2:3, :] * s_ref[pl.ds(pad, tm), :])

        act[slot][...] = (_silu(conv(0, sg[slot])) * conv(1, sv[slot])).astype(BF16)

    def down_proj(j, slot):
        acc_ref[...] += _dot(act[slot][...], wd_ref[j])

    def steady(j, slot):
        up_proj(j + 1, 1 - slot)
        conv_act(j, slot)
        down_proj(j, slot)

    assert nj % 2 == 1
    up_proj(0, 0)

    def pair(p, carry):
        steady(2 * p, 0)
        steady(2 * p + 1, 1)
        return carry

    lax.fori_loop(0, (nj - 1) // 2, pair, 0)
    conv_act(nj - 1, 0)
    down_proj(nj - 1, 0)
    out = acc_ref[...]
    if final_norm:
        out = _rms(out, fin_ref[...])
    o_ref[...] = out


def _out_ffn(h, ya, yb, yc, params, fin, layer, s, tm, final_norm):
    t = h.shape[0]
    _, _, nj, _, tf = params[4].shape
    row = lambda n: pl.BlockSpec((tm, n), lambda i: (i, 0))
    return pl.pallas_call(
        functools.partial(_out_ffn_kernel, tm=tm, tiles_per_seq=s // tm, final_norm=final_norm),
        grid=(t // tm,),
        in_specs=[row(D_MODEL), row(SB_WIDTH), row(SSM_INNER), row(MLA_WIDTH)]
                 + [_layer_spec(a, layer) for a in params]
                 + [pl.BlockSpec(fin.shape, lambda i: (0, 0), pipeline_mode=pl.Buffered(1))],
        out_specs=row(D_MODEL),
        out_shape=jax.ShapeDtypeStruct((t, D_MODEL), F32),
        scratch_shapes=[pltpu.VMEM((tm, D_MODEL), BF16), pltpu.VMEM((tm, D_MODEL), F32),
                        pltpu.VMEM((nj, SUBLANES, tf), F32), pltpu.VMEM((nj, SUBLANES, tf), F32),
                        ] + [pltpu.VMEM((SUBLANES + tm, tf), F32)] * 4 + [pltpu.VMEM((tm, tf), BF16)] * 2,
        compiler_params=pltpu.CompilerParams(
            dimension_semantics=("arbitrary",), vmem_limit_bytes=VMEM_LIMIT_FFN),
        name="out_ffn",
    )(h, ya, yb, yc, *params, fin)


def _zeros_like_cols(a, n):
    return jnp.zeros(a.shape[:-1] + (n,), a.dtype)


def _pack_in_proj(w):
    o_dt = C_CQ
    o_cq = o_dt + SSM_HEADS
    o_kr = o_cq + MLA_Q_RANK + MLA_KV_RANK
    kr = w[..., o_kr:o_kr + MLA_ROPE]
    z16, z32 = _zeros_like_cols(w, 16), _zeros_like_cols(w, 32)
    tail = jnp.concatenate([w[..., o_cq:o_kr], z32, kr[..., :HALF], z16, z32, kr[..., HALF:], z16,
                            w[..., o_dt:o_cq], _zeros_like_cols(w, LANES - SSM_HEADS)], axis=-1)
    return w[..., :o_dt].astype(BF16), tail.astype(BF16)


def _pack_uq(w):
    l, r, _ = w.shape
    w = w.reshape(l, r, MLA_HEADS, MLA_NOPE + MLA_ROPE)
    nope, rp = w[..., :MLA_NOPE], w[..., MLA_NOPE:]
    z16 = _zeros_like_cols(w, 16)
    out = jnp.concatenate([nope[..., :32], rp[..., :HALF], z16, nope[..., 32:], rp[..., HALF:], z16],
                          axis=-1)
    return out.reshape(l, r, MLA_HEADS * MLA_PAD).astype(BF16)


def _pack_ukv(w):
    l, r, _ = w.shape
    w = w.reshape(l, r, MLA_HEADS, MLA_NOPE + MLA_V)
    nope, wv = w[..., :MLA_NOPE], w[..., MLA_NOPE:]
    z32, z64 = _zeros_like_cols(w, 32), _zeros_like_cols(w, LANES - MLA_V)
    wk = jnp.concatenate([nope[..., :32], z32, nope[..., 32:], z32], axis=-1)
    odd = (jnp.arange(MLA_HEADS) % 2 == 1)[None, None, :, None]
    wv = jnp.where(odd, jnp.concatenate([z64, wv], axis=-1), jnp.concatenate([wv, z64], axis=-1))
    flat = lambda a: a.reshape(l, r, MLA_HEADS * LANES).astype(BF16)
    return flat(wk), flat(wv)


def _halves_chunks(a, tf):
    l, r, _ = a.shape
    return a.reshape(l, r, 2, D_FF // tf, tf).transpose(0, 2, 3, 1, 4)


def _row(v, width=None):
    if width is not None and width > v.shape[-1]:
        v = jnp.concatenate([v, _zeros_like_cols(v, width - v.shape[-1])], axis=-1)
    return v[:, None, :]


TM_ROPE = 512
TM_IN = 1024
TQ_SB = 256
TQ_MLA = 1024
TK_MLA = 512
SSD_BLOCK_CHUNKS = 4
TM_FFN = 1024
TF_FFN = 256


def kernel(x, positions, mix_norm, w_in, sb_out_norm, ssm_conv_w, ssm_conv_b, ssm_dt_bias, ssm_a_log, ssm_d, ssm_out_norm, mla_q_norm, mla_w_uq, mla_kv_norm, mla_w_ukv, mla_out_norm, w_out, ffn_norm, ffn_w_up, ffn_conv_w, ffn_conv_b, ffn_w_down, final_norm):
    b, s, d = x.shape
    t = b * s
    depth = w_in.shape[0]
    h = x.reshape(t, d)

    inv_freq = 1.0 / (ROPE_THETA ** (jnp.arange(0, MLA_ROPE, 2, dtype=F32) / MLA_ROPE))
    per_row = LANES // HALF
    pos_dense = jnp.repeat(positions.reshape(t // per_row, per_row), HALF, axis=1)
    cos_d, sin_d, nsin_d = _rope_tables(pos_dense, jnp.tile(inv_freq, per_row)[None, :],
                                        TM_ROPE)
    cos_h, sin_h, nsin_h = (a.reshape(t, HALF) for a in (cos_d, sin_d, nsin_d))
    one16, one32 = jnp.ones((t, 16), F32), jnp.ones((t, 32), F32)
    z16, z32 = jnp.zeros((t, 16), F32), jnp.zeros((t, 32), F32)
    cos_t = jnp.concatenate([one32, cos_h, one16, one32, cos_h, one16], axis=1)
    sin_t = jnp.concatenate([z32, nsin_h, z16, z32, sin_h, z16], axis=1)

    w_main, w_tail = _pack_in_proj(w_in)
    w_uk, w_uv = _pack_ukv(mla_w_ukv)
    in_params = (_row(mix_norm), w_main, w_tail, _row(mla_q_norm), _pack_uq(mla_w_uq),
                 _row(mla_kv_norm), w_uk, w_uv)
    ssd_params = (ssm_conv_w, _row(ssm_conv_b), _row(ssm_dt_bias, LANES), _row(ssm_a_log, LANES),
                  _row(jnp.repeat(ssm_d, SSM_HEAD_DIM, axis=-1)), _row(ssm_out_norm))
    out_params = (_row(sb_out_norm), _row(mla_out_norm), w_out.astype(BF16), _row(ffn_norm),
                  _halves_chunks(ffn_w_up, TF_FFN).astype(BF16), _halves_chunks(ffn_conv_w, TF_FFN),
                  _halves_chunks(_row(ffn_conv_b), TF_FFN),
                  ffn_w_down.astype(BF16).reshape(depth, D_FF // TF_FFN, TF_FFN, D_MODEL))

    for i in range(depth):
        sb, z, xbc, dt, qm, km, vm = _mix_in(h, in_params, cos_t, sin_t, i, TM_IN)
        ya = _sb_attn(sb, b, s, TQ_SB)
        yb = _ssd(xbc, z, dt, ssd_params, i, b, s, SSD_BLOCK_CHUNKS)
        yc = _mla_attn(qm, km, vm, b, s, TQ_MLA, TK_MLA)
        h = _out_ffn(h, ya, yb, yc, out_params, final_norm[None, :], i, s, TM_FFN,
                     final_norm=(i == depth - 1))
    return h.reshape(b, s, d)
```

```python
import functools

import jax
import jax.numpy as jnp
from jax import lax
from jax.experimental import pallas as pl
from jax.experimental.pallas import tpu as pltpu

F32 = jnp.float32
BF16 = jnp.bfloat16

D_MODEL = 1024
EPS = 1e-6
SB_HEADS = 4
SB_HEAD_DIM = 64
SB_WIDTH = SB_HEADS * SB_HEAD_DIM
SSM_HEADS = 8
SSM_HEAD_DIM = 64
SSM_INNER = SSM_HEADS * SSM_HEAD_DIM
SSM_GROUPS = 2
SSM_STATE = 64
SSM_CONV = 4
SSM_CHUNK = 128
SSM_CONV_DIM = SSM_INNER + 2 * SSM_GROUPS * SSM_STATE
MLA_HEADS = 4
MLA_NOPE = 64
MLA_ROPE = 32
MLA_V = 64
MLA_Q_RANK = 256
MLA_KV_RANK = 128
MLA_WIDTH = MLA_HEADS * MLA_V
ROPE_THETA = 10000.0
D_MIX = SB_WIDTH + SSM_INNER + MLA_WIDTH
D_FF = 2816
FFN_CONV = 3

LANES = 128
SUBLANES = 8
MLA_PAD = 128
HALF = MLA_ROPE // 2
VMEM_LIMIT = 48 * 1024 * 1024
VMEM_LIMIT_FFN = 62 * 1024 * 1024

C_SB = 0
C_Z = C_SB + 3 * SB_WIDTH
C_XBC = C_Z + SSM_INNER
C_CQ = C_XBC + SSM_CONV_DIM
C_CKV = C_CQ + MLA_Q_RANK
C_KPE = C_CKV + MLA_KV_RANK
C_DT = C_KPE + MLA_PAD
N_IN = C_DT + LANES

MLA_Q_SCALE = (MLA_NOPE + MLA_ROPE) ** -0.5 * 1.4426950408889634

SB_Q_SCALE = SB_HEAD_DIM ** -0.5 * 1.4426950408889634
SB_LOG2_FLOOR = -151.0


def _rms(x, g):
    return x * lax.rsqrt(jnp.mean(x * x, axis=-1, keepdims=True) + EPS) * g


def _silu(x):
    return x * (1.0 / (1.0 + jnp.exp(-x)))


def _softplus(x):
    return jnp.maximum(x, 0.0) + jnp.log1p(jnp.exp(-jnp.abs(x)))


def _split3(x):
    a = x.astype(BF16)
    r = x - a.astype(F32)
    b = r.astype(BF16)
    c = (r - b.astype(F32)).astype(BF16)
    return a, b, c


def _lane_tile(x, n):
    return jnp.concatenate([x] * n, axis=1)


def _dot(a, b):
    return jnp.dot(a, b, preferred_element_type=F32)


def _dot_nt(a, b):
    return lax.dot_general(a, b, (((1,), (1,)), ((), ())), preferred_element_type=F32)


def _layer_spec(a, layer):
    rest = a.shape[1:]
    return pl.BlockSpec((None,) + rest, lambda *_: (layer,) + (0,) * len(rest),
                        pipeline_mode=pl.Buffered(1))


def _rope_table_kernel(pos_ref, invf_ref, sign_ref, cos_ref, sin_ref):
    ang = pos_ref[...].astype(F32) * invf_ref[...]
    cos_ref[...] = jnp.cos(ang)
    sin_ref[...] = jnp.sin(ang) * sign_ref[...]


def _rope_tables(pos_col, invf_lane, sign_lane, tm):
    t = pos_col.shape[0]
    return pl.pallas_call(
        _rope_table_kernel,
        grid=(t // tm,),
        in_specs=[pl.BlockSpec((tm, 1), lambda i: (i, 0)),
                  pl.BlockSpec((1, LANES), lambda i: (0, 0)),
                  pl.BlockSpec((1, LANES), lambda i: (0, 0))],
        out_specs=[pl.BlockSpec((tm, LANES), lambda i: (i, 0)),
                   pl.BlockSpec((tm, LANES), lambda i: (i, 0))],
        out_shape=[jax.ShapeDtypeStruct((t, LANES), F32)] * 2,
        compiler_params=pltpu.CompilerParams(dimension_semantics=("parallel",)),
        name="rope_tables",
    )(pos_col, invf_lane, sign_lane)


def _mix_in_kernel(h_ref, g_ref, w_ref, wt_ref, qn_ref, wuq_ref, kvn_ref, wk_ref, wv_ref, cos_ref,
                   sin_ref, sb_ref, z_ref, xbc_ref, dt_ref, qm_ref, km_ref, vm_ref):
    xn = _rms(h_ref[...], g_ref[...]).astype(BF16)

    def proj(lo, hi):
        if lo >= C_CQ:
            return _dot(xn, wt_ref[:, lo - C_CQ:hi - C_CQ])
        return _dot(xn, w_ref[:, lo:hi])

    cos = cos_ref[...]
    sin = sin_ref[...]

    def rope(x):
        return x * cos + pltpu.roll(x, LANES // 2, axis=1) * sin

    cq_raw = proj(C_CQ, C_CKV)
    ckv_raw = proj(C_CKV, C_KPE)
    kpe_raw = proj(C_KPE, C_DT)
    sb = proj(C_SB, C_Z)
    sb_ref[:, :SB_WIDTH] = (sb[:, :SB_WIDTH] * SB_Q_SCALE).astype(BF16)
    sb_ref[:, SB_WIDTH:] = sb[:, SB_WIDTH:].astype(BF16)

    cq = _rms(cq_raw, qn_ref[...]).astype(BF16)
    ckv = _rms(ckv_raw, kvn_ref[...]).astype(BF16)
    qf = _dot(cq, wuq_ref[...])
    kn = _dot(ckv, wk_ref[...])
    lane = lax.broadcasted_iota(jnp.int32, (1, MLA_HEADS * LANES), 1)
    ones = ((lane % LANES < MLA_V) == ((lane // LANES) % 2 == 1)).astype(F32)
    vm_ref[...] = (_dot(ckv, wv_ref[...]) + ones).astype(BF16)

    z_ref[...] = proj(C_Z, C_XBC)
    xbc_ref[...] = proj(C_XBC, C_CQ)
    dt_ref[...] = proj(C_DT, N_IN)

    kpe = rope(kpe_raw)
    for h in range(MLA_HEADS):
        sl = slice(h * MLA_PAD, (h + 1) * MLA_PAD)
        qm_ref[:, sl] = (rope(qf[:, sl]) * MLA_Q_SCALE).astype(BF16)
        km_ref[:, sl] = (kn[:, sl] + kpe).astype(BF16)


def _mix_in(h, params, cos_t, sin_t, layer, tm):
    t = h.shape[0]
    row = lambda n: pl.BlockSpec((tm, n), lambda i: (i, 0))
    out_widths = [(3 * SB_WIDTH, BF16), (SSM_INNER, F32), (SSM_CONV_DIM, F32), (LANES, F32),
                  (MLA_HEADS * MLA_PAD, BF16), (MLA_HEADS * MLA_PAD, BF16),
                  (MLA_HEADS * LANES, BF16)]
    return pl.pallas_call(
        _mix_in_kernel,
        grid=(t // tm,),
        in_specs=[row(D_MODEL)] + [_layer_spec(a, layer) for a in params] + [row(LANES), row(LANES)],
        out_specs=[row(n) for n, _ in out_widths],
        out_shape=[jax.ShapeDtypeStruct((t, n), d) for n, d in out_widths],
        compiler_params=pltpu.CompilerParams(dimension_semantics=("parallel",),
                                             vmem_limit_bytes=VMEM_LIMIT),
        name="mix_in",
    )(h, *params, cos_t, sin_t)


def _sb_kernel(q_ref, k_ref, v_ref, o_ref, c_ref, acc_ref, *, tq):
    tk = tq
    qi = pl.program_id(1)
    upper = (lax.broadcasted_iota(jnp.int32, (tk, tk), 0) >
             lax.broadcasted_iota(jnp.int32, (tk, tk), 1)).astype(BF16)
    lane = lax.broadcasted_iota(jnp.int32, (tq, LANES), 1)
    c_ref[...] = jnp.zeros(c_ref.shape, F32)
    acc_ref[...] = jnp.zeros(acc_ref.shape, F32)

    heads = range(SB_HEADS)
    pair = [slice((h // 2) * LANES, (h // 2 + 1) * LANES) for h in heads]

    def step(j, masked):
        ks = pl.multiple_of(j * tk, tk)
        z = []
        for h in heads:
            own = (lane < SB_HEAD_DIM) if h % 2 == 0 else (lane >= SB_HEAD_DIM)
            qh = jnp.where(own, q_ref[:, pair[h]], jnp.zeros((), BF16))
            z.append(_dot_nt(qh, k_ref[pl.ds(ks, tk), pair[h]]))
        log_keep = [-(jnp.maximum(x, 0.0) + jnp.log2(1.0 + jnp.exp2(-jnp.abs(x))))
                    for x in z]
        if masked:
            mask = (lax.broadcasted_iota(jnp.int32, (tq, tk), 1) <
                    lax.broadcasted_iota(jnp.int32, (tq, tk), 0))
            log_keep = [jnp.where(mask, x, 0.0) for x in log_keep]
        hi = [x.astype(BF16) for x in log_keep]
        lo = [(log_keep[h] - hi[h].astype(F32)).astype(BF16) for h in heads]
        tail = [_dot(hi[h], upper) + _dot(lo[h], upper) for h in heads]
        c_all = None
        pv = []
        for h in heads:
            c = c_ref[h]
            later = tail[h] + _lane_tile(c, tk // LANES)
            w = jnp.exp2(z[h] + log_keep[h] + later)
            if masked:
                w = jnp.where(mask, w, 0.0)
            pv.append(_dot(w.astype(BF16), v_ref[pl.ds(ks, tk), pair[h]]))
            c = c + jnp.sum(log_keep[h], axis=1, keepdims=True)
            c_ref[h] = c
            c_all = c if c_all is None else jnp.maximum(c_all, c)
        for h in heads:
            acc_ref[h] += pv[h]
        return jnp.max(c_all)

    def cond(carry):
        j, c_max = carry
        return jnp.logical_and(j >= 0, c_max > SB_LOG2_FLOOR)

    def body(carry):
        j, _ = carry
        return j - 1, step(j, False)

    lax.while_loop(cond, body, (qi - 1, step(qi, True)))
    for hp in range(SB_HEADS // 2):
        o_ref[:, hp * LANES:(hp + 1) * LANES] = jnp.where(lane < SB_HEAD_DIM, acc_ref[2 * hp],
                                                          acc_ref[2 * hp + 1])


def _sb_attn(sb, b, s, tq):
    nq = s // tq
    return pl.pallas_call(
        functools.partial(_sb_kernel, tq=tq),
        grid=(b, nq),
        in_specs=[pl.BlockSpec((tq, SB_WIDTH), lambda bi, qi: (bi * nq + qi, 0)),
                  pl.BlockSpec((s, SB_WIDTH), lambda bi, qi: (bi, 1)),
                  pl.BlockSpec((s, SB_WIDTH), lambda bi, qi: (bi, 2))],
        out_specs=pl.BlockSpec((tq, SB_WIDTH), lambda bi, qi: (bi * nq + qi, 0)),
        out_shape=jax.ShapeDtypeStruct((b * s, SB_WIDTH), F32),
        scratch_shapes=[pltpu.VMEM((SB_HEADS, tq, LANES), F32),
                        pltpu.VMEM((SB_HEADS, tq, LANES), F32)],
        compiler_params=pltpu.CompilerParams(dimension_semantics=("parallel", "parallel"),
                                             vmem_limit_bytes=VMEM_LIMIT),
        name="sb_attn",
    )(sb, sb, sb)


def _ssd_kernel(xbc_ref, z_ref, dt_ref, cw_ref, cb_ref, dtb_ref, alog_ref, dl_ref, gn_ref,
                y_ref, tail_ref, xs_ref, state_ref, *, nch):
    l = SSM_CHUNK
    rows = nch * l
    hd = SSM_HEAD_DIM
    pairs = SSM_HEADS // 2

    @pl.when(pl.program_id(1) == 0)
    def _():
        tail_ref[...] = jnp.zeros_like(tail_ref)
        state_ref[...] = jnp.zeros_like(state_ref)

    xs_ref[0:SUBLANES, :] = tail_ref[...]
    xs_ref[SUBLANES:SUBLANES + rows, :] = xbc_ref[...]
    tail_ref[...] = xbc_ref[rows - SUBLANES:rows, :]
    conv = cb_ref[...]
    for k in range(SSM_CONV):
        conv = conv + cw_ref[k:k + 1, :] * xs_ref[pl.ds(SUBLANES - (SSM_CONV - 1) + k, rows), :]
    xa = _silu(conv)
    dt_all = _softplus(dt_ref[...] + dtb_ref[...])
    da_all = dt_all * (-jnp.exp(alog_ref[...]))

    tril = (lax.broadcasted_iota(jnp.int32, (l, l), 0) >=
            lax.broadcasted_iota(jnp.int32, (l, l), 1))
    trilb = tril.astype(BF16)
    lo_half = lax.broadcasted_iota(jnp.int32, (l, LANES), 1) < hd
    lo_half_s = lax.broadcasted_iota(jnp.int32, (SSM_STATE, LANES), 1) < hd
    gn = SSM_GROUPS * SSM_STATE

    for c in range(nch):
        rs = slice(c * l, (c + 1) * l)
        x = xa[rs, :SSM_INNER]
        xb = x.astype(BF16)
        bm = xa[rs, SSM_INNER:SSM_INNER + gn]
        cm = xa[rs, SSM_INNER + gn:]
        d1, d2, d3 = _split3(da_all[rs])
        cs = _dot(trilb, d1) + _dot(trilb, d2) + _dot(trilb, d3)
        cs8 = cs.T[0:SSM_HEADS]
        dt8 = dt_all[rs].T[0:SSM_HEADS]
        bm_t = bm.T
        end8 = cs8[:, l - 1:l]
        w8 = jnp.exp(end8 - cs8) * dt8
        dec8 = jnp.broadcast_to(jnp.exp(end8), (SSM_HEADS, LANES))
        cg = [cm[:, g * SSM_STATE:(g + 1) * SSM_STATE].astype(BF16) for g in range(SSM_GROUPS)]
        bg_t = [bm_t[g * SSM_STATE:(g + 1) * SSM_STATE] for g in range(SSM_GROUPS)]
        cb = [_dot(cg[g], bg_t[g].astype(BF16)) for g in range(SSM_GROUPS)]

        ys = []
        for p in range(pairs):
            g = (2 * p) // (SSM_HEADS // SSM_GROUPS)
            ps = slice(p * LANES, (p + 1) * LANES)
            xp = xb[:, ps]
            y_diag, contrib, col = [], [], []
            for h in (2 * p, 2 * p + 1):
                cs_col = jnp.broadcast_to(cs[:, h:h + 1], (l, l))
                decay = jnp.exp(cs_col - cs8[h:h + 1, :])
                m = jnp.where(tril, cb[g] * decay * dt8[h:h + 1, :], 0.0)
                y_diag.append(_dot(m.astype(BF16), xp))
                contrib.append(_dot((bg_t[g] * w8[h:h + 1, :]).astype(BF16), xp))
                col.append(cs_col)
            st = state_ref[p]
            y_off = _dot(cg[g], st.astype(BF16)) * jnp.exp(jnp.where(lo_half, col[0], col[1]))
            st_decay = jnp.where(lo_half_s, dec8[2 * p:2 * p + 1, :], dec8[2 * p + 1:2 * p + 2, :])
            state_ref[p] = st * st_decay + jnp.where(lo_half_s, contrib[0], contrib[1])
            ys.append(jnp.where(lo_half, y_diag[0], y_diag[1]) + y_off + x[:, ps] * dl_ref[:, ps])
        y = jnp.concatenate(ys, axis=1) * _silu(z_ref[rs, :])
        y_ref[rs, :] = _rms(y, gn_ref[...]).astype(BF16)


def _ssd(xbc, z, dt, params, layer, b, s, nch):
    rows = nch * SSM_CHUNK
    nb = s // rows
    row = lambda n: pl.BlockSpec((rows, n), lambda bi, ci: (bi * nb + ci, 0))
    return pl.pallas_call(
        functools.partial(_ssd_kernel, nch=nch),
        grid=(b, nb),
        in_specs=[row(SSM_CONV_DIM), row(SSM_INNER), row(LANES)]
                 + [_layer_spec(a, layer) for a in params],
        out_specs=row(SSM_INNER),
        out_shape=jax.ShapeDtypeStruct((b * s, SSM_INNER), BF16),
        scratch_shapes=[pltpu.VMEM((SUBLANES, SSM_CONV_DIM), F32),
                        pltpu.VMEM((SUBLANES + rows, SSM_CONV_DIM), F32),
                        pltpu.VMEM((SSM_HEADS // 2, SSM_STATE, LANES), F32)],
        compiler_params=pltpu.CompilerParams(dimension_semantics=("parallel", "arbitrary"),
                                             vmem_limit_bytes=VMEM_LIMIT),
        name="ssd",
    )(xbc, z, dt, *params)


def _mla_kernel(q_ref, k_ref, v_ref, o_ref, m_ref, acc_ref, *, tq, tk):
    qi = pl.program_id(1)
    diag_tiles = tq // tk
    m_ref[...] = jnp.full(m_ref.shape, -1e30, F32)
    acc_ref[...] = jnp.zeros(acc_ref.shape, F32)

    heads = range(MLA_HEADS)

    def step(j, diag):
        ks = pl.multiple_of(j * tk, tk)
        r0 = 0 if diag is None else diag * tk
        rs = slice(r0, tq)
        nr = tq - r0
        sc = []
        for h in heads:
            hs = slice(h * MLA_PAD, (h + 1) * MLA_PAD)
            sc.append(_dot_nt(q_ref[rs, hs], k_ref[pl.ds(ks, tk), hs]))
        if diag is not None:
            keep = (lax.broadcasted_iota(jnp.int32, (nr, tk), 1) <=
                    lax.broadcasted_iota(jnp.int32, (nr, tk), 0))
            sc = [jnp.where(keep, x, -jnp.inf) for x in sc]
        alpha, pv = [], []
        for h in heads:
            m_old = m_ref[h, rs, :]
            m_new = jnp.maximum(m_old, jnp.max(sc[h], axis=1, keepdims=True))
            p = jnp.exp2(sc[h] - _lane_tile(m_new, tk // LANES))
            m_ref[h, rs, :] = m_new
            alpha.append(jnp.exp2(m_old - m_new))
            pv.append(_dot(p.astype(BF16), v_ref[pl.ds(ks, tk), h * LANES:(h + 1) * LANES]))
        for h in heads:
            acc_ref[h, rs, :] = alpha[h] * acc_ref[h, rs, :] + pv[h]

    def body(j, carry):
        step(j, None)
        return carry

    lax.fori_loop(0, qi * diag_tiles, body, 0)
    for u in range(diag_tiles):
        step(qi * diag_tiles + u, u)
    lane = lax.broadcasted_iota(jnp.int32, (tq, LANES), 1)
    for hp in range(MLA_HEADS // 2):
        even = acc_ref[2 * hp]
        odd = acc_ref[2 * hp + 1]
        even = even / pltpu.roll(even, LANES // 2, axis=1)
        odd = odd / pltpu.roll(odd, LANES // 2, axis=1)
        o_ref[:, hp * LANES:(hp + 1) * LANES] = jnp.where(lane < MLA_V, even, odd)


def _mla_attn(qm, km, vm, b, s, tq, tk):
    nq = s // tq
    return pl.pallas_call(
        functools.partial(_mla_kernel, tq=tq, tk=tk),
        grid=(b, nq),
        in_specs=[pl.BlockSpec((tq, MLA_HEADS * MLA_PAD), lambda bi, qi: (bi * nq + qi, 0)),
                  pl.BlockSpec((s, MLA_HEADS * MLA_PAD), lambda bi, qi: (bi, 0),
                               pipeline_mode=pl.Buffered(1)),
                  pl.BlockSpec((s, MLA_HEADS * LANES), lambda bi, qi: (bi, 0),
                               pipeline_mode=pl.Buffered(1))],
        out_specs=pl.BlockSpec((tq, MLA_WIDTH), lambda bi, qi: (bi * nq + qi, 0)),
        out_shape=jax.ShapeDtypeStruct((b * s, MLA_WIDTH), F32),
        scratch_shapes=[pltpu.VMEM((MLA_HEADS, tq, LANES), F32),
                        pltpu.VMEM((MLA_HEADS, tq, LANES), F32)],
        compiler_params=pltpu.CompilerParams(dimension_semantics=("parallel", "parallel"),
                                             vmem_limit_bytes=VMEM_LIMIT),
        name="mla_attn",
    )(qm, km, vm)


def _out_ffn_kernel(h_ref, ya_ref, yb_ref, yc_ref, ga_ref, gc_ref, wo_ref, g_ref, wup_ref, cw_ref,
                    cb_ref, wd_ref, fin_ref,
                    o_ref, hn_ref, acc_ref, carry_g, carry_v, sg0, sg1, sv0, sv1, act0, act1, *, tm,
                    tiles_per_seq, final_norm):
    c1 = SB_WIDTH
    c2 = SB_WIDTH + SSM_INNER
    h1 = h_ref[...] + _dot(yb_ref[...], wo_ref[c1:c2, :])
    ya = _rms(ya_ref[...], ga_ref[...]).astype(BF16)
    yc = _rms(yc_ref[...], gc_ref[...]).astype(BF16)
    h1 = h1 + _dot(ya, wo_ref[:c1, :]) + _dot(yc, wo_ref[c2:, :])
    hn_ref[...] = _rms(h1, g_ref[...]).astype(BF16)
    acc_ref[...] = h1

    first = (pl.program_id(0) % tiles_per_seq) == 0
    pad = SUBLANES
    tf = sg0.shape[1]
    nj = D_FF // tf

    def cols(half, j):
        return pl.ds(pl.multiple_of(half * D_FF + j * tf, tf), tf)

    sg, sv, act = (sg0, sg1), (sv0, sv1), (act0, act1)

    def up_proj(j, slot):
        for half, carry_ref, s_ref in ((0, carry_g, sg[slot]), (1, carry_v, sv[slot])):
            up = _dot(hn_ref[...], wup_ref[:, cols(half, j)])
            s_ref[0:pad, :] = jnp.where(first, 0.0, carry_ref[j])
            s_ref[pad:pad + tm, :] = up
            carry_ref[j] = up[tm - pad:tm, :]

    def conv_act(j, slot):
        def conv(half, s_ref):
            cw = cw_ref[:, cols(half, j)]
            return (cb_ref[:, cols(half, j)] + cw[0:1, :] * s_ref[pl.ds(pad - 2, tm), :]
                    + cw[1:2, :] * s_ref[pl.ds(pad - 1, tm), :]
                    + cw[2:3, :] * s_ref[pl.ds(pad, tm), :])

        act[slot][...] = (_silu(conv(0, sg[slot])) * conv(1, sv[slot])).astype(BF16)

    def down_proj(j, slot):
        acc_ref[...] += _dot(act[slot][...], wd_ref[pl.ds(pl.multiple_of(j * tf, tf), tf), :])

    def steady(j, slot):
        up_proj(j + 1, 1 - slot)
        conv_act(j, slot)
        down_proj(j, slot)

    assert nj % 2 == 1
    up_proj(0, 0)

    def pair(p, carry):
        steady(2 * p, 0)
        steady(2 * p + 1, 1)
        return carry

    lax.fori_loop(0, (nj - 1) // 2, pair, 0)
    conv_act(nj - 1, 0)
    down_proj(nj - 1, 0)
    out = acc_ref[...]
    if final_norm:
        out = _rms(out, fin_ref[...])
    o_ref[...] = out


def _out_ffn(h, ya, yb, yc, params, fin, layer, s, tm, tf, final_norm):
    t = h.shape[0]
    nj = D_FF // tf
    row = lambda n: pl.BlockSpec((tm, n), lambda i: (i, 0))
    return pl.pallas_call(
        functools.partial(_out_ffn_kernel, tm=tm, tiles_per_seq=s // tm, final_norm=final_norm),
        grid=(t // tm,),
        in_specs=[row(D_MODEL), row(SB_WIDTH), row(SSM_INNER), row(MLA_WIDTH)]
                 + [_layer_spec(a, layer) for a in params]
                 + [pl.BlockSpec(fin.shape, lambda i: (0, 0), pipeline_mode=pl.Buffered(1))],
        out_specs=row(D_MODEL),
        out_shape=jax.ShapeDtypeStruct((t, D_MODEL), F32),
        scratch_shapes=[pltpu.VMEM((tm, D_MODEL), BF16), pltpu.VMEM((tm, D_MODEL), F32),
                        pltpu.VMEM((nj, SUBLANES, tf), F32), pltpu.VMEM((nj, SUBLANES, tf), F32),
                        ] + [pltpu.VMEM((SUBLANES + tm, tf), F32)] * 4 + [pltpu.VMEM((tm, tf), BF16)] * 2,
        compiler_params=pltpu.CompilerParams(
            dimension_semantics=("arbitrary",), vmem_limit_bytes=VMEM_LIMIT_FFN),
        name="out_ffn",
    )(h, ya, yb, yc, *params, fin)


def _zeros_like_cols(a, n):
    return jnp.zeros(a.shape[:-1] + (n,), a.dtype)


def _pack_in_proj(w):
    o_dt = C_CQ
    o_cq = o_dt + SSM_HEADS
    o_kr = o_cq + MLA_Q_RANK + MLA_KV_RANK
    kr = w[..., o_kr:o_kr + MLA_ROPE]
    z16, z32 = _zeros_like_cols(w, 16), _zeros_like_cols(w, 32)
    tail = jnp.concatenate([w[..., o_cq:o_kr], z32, kr[..., :HALF], z16, z32, kr[..., HALF:], z16,
                            w[..., o_dt:o_cq], _zeros_like_cols(w, LANES - SSM_HEADS)], axis=-1)
    return w[..., :o_dt].astype(BF16), tail.astype(BF16)


def _pack_uq(w):
    l, r, _ = w.shape
    w = w.reshape(l, r, MLA_HEADS, MLA_NOPE + MLA_ROPE)
    nope, rp = w[..., :MLA_NOPE], w[..., MLA_NOPE:]
    z16 = _zeros_like_cols(w, 16)
    out = jnp.concatenate([nope[..., :32], rp[..., :HALF], z16, nope[..., 32:], rp[..., HALF:], z16],
                          axis=-1)
    return out.reshape(l, r, MLA_HEADS * MLA_PAD).astype(BF16)


def _pack_ukv(w):
    l, r, _ = w.shape
    w = w.reshape(l, r, MLA_HEADS, MLA_NOPE + MLA_V)
    nope, wv = w[..., :MLA_NOPE], w[..., MLA_NOPE:]
    z32, z64 = _zeros_like_cols(w, 32), _zeros_like_cols(w, LANES - MLA_V)
    wk = jnp.concatenate([nope[..., :32], z32, nope[..., 32:], z32], axis=-1)
    odd = (jnp.arange(MLA_HEADS) % 2 == 1)[None, None, :, None]
    wv = jnp.where(odd, jnp.concatenate([z64, wv], axis=-1), jnp.concatenate([wv, z64], axis=-1))
    flat = lambda a: a.reshape(l, r, MLA_HEADS * LANES).astype(BF16)
    return flat(wk), flat(wv)


def _row(v, width=None):
    if width is not None and width > v.shape[-1]:
        v = jnp.concatenate([v, _zeros_like_cols(v, width - v.shape[-1])], axis=-1)
    return v[:, None, :]


TM_ROPE = 1024
TM_IN = 1024
TQ_SB = 256
TQ_MLA = 1024
TK_MLA = 512
SSD_BLOCK_CHUNKS = 4
TM_FFN = 1024
TF_FFN = 256


def kernel(x, positions, mix_norm, w_in, sb_out_norm, ssm_conv_w, ssm_conv_b, ssm_dt_bias, ssm_a_log, ssm_d, ssm_out_norm, mla_q_norm, mla_w_uq, mla_kv_norm, mla_w_ukv, mla_out_norm, w_out, ffn_norm, ffn_w_up, ffn_conv_w, ffn_conv_b, ffn_w_down, final_norm):
    b, s, d = x.shape
    t = b * s
    depth = w_in.shape[0]
    h = x.reshape(t, d)

    inv_freq = 1.0 / (ROPE_THETA ** (jnp.arange(0, MLA_ROPE, 2, dtype=F32) / MLA_ROPE))
    z16, z32 = jnp.zeros((16,), F32), jnp.zeros((32,), F32)
    invf_lane = jnp.concatenate([z32, inv_freq, z16, z32, inv_freq, z16])[None, :]
    sign_lane = jnp.concatenate([z32, -jnp.ones((16,), F32), z16, z32, jnp.ones((16,), F32), z16])[None, :]
    cos_t, sin_t = _rope_tables(positions.reshape(t, 1), invf_lane, sign_lane, TM_ROPE)

    w_main, w_tail = _pack_in_proj(w_in)
    w_uk, w_uv = _pack_ukv(mla_w_ukv)
    in_params = (_row(mix_norm), w_main, w_tail, _row(mla_q_norm), _pack_uq(mla_w_uq),
                 _row(mla_kv_norm), w_uk, w_uv)
    ssd_params = (ssm_conv_w, _row(ssm_conv_b), _row(ssm_dt_bias, LANES), _row(ssm_a_log, LANES),
                  _row(jnp.repeat(ssm_d, SSM_HEAD_DIM, axis=-1)), _row(ssm_out_norm))
    out_params = (_row(sb_out_norm), _row(mla_out_norm), w_out.astype(BF16), _row(ffn_norm),
                  ffn_w_up.astype(BF16), ffn_conv_w, _row(ffn_conv_b), ffn_w_down.astype(BF16))

    for i in range(depth):
        sb, z, xbc, dt, qm, km, vm = _mix_in(h, in_params, cos_t, sin_t, i, TM_IN)
        ya = _sb_attn(sb, b, s, TQ_SB)
        yb = _ssd(xbc, z, dt, ssd_params, i, b, s, SSD_BLOCK_CHUNKS)
        yc = _mla_attn(qm, km, vm, b, s, TQ_MLA, TK_MLA)
        h = _out_ffn(h, ya, yb, yc, out_params, final_norm[None, :], i, s, TM_FFN, TF_FFN,
                     final_norm=(i == depth - 1))
    return h.reshape(b, s, d)
```

```python
import functools

import jax
import jax.numpy as jnp
from jax import lax
from jax.experimental import pallas as pl
from jax.experimental.pallas import tpu as pltpu

F32 = jnp.float32
BF16 = jnp.bfloat16

D_MODEL = 1024
EPS = 1e-6
SB_HEADS = 4
SB_HEAD_DIM = 64
SB_WIDTH = SB_HEADS * SB_HEAD_DIM
SSM_HEADS = 8
SSM_HEAD_DIM = 64
SSM_INNER = SSM_HEADS * SSM_HEAD_DIM
SSM_GROUPS = 2
SSM_STATE = 64
SSM_CONV = 4
SSM_CHUNK = 128
SSM_CONV_DIM = SSM_INNER + 2 * SSM_GROUPS * SSM_STATE
MLA_HEADS = 4
MLA_NOPE = 64
MLA_ROPE = 32
MLA_V = 64
MLA_Q_RANK = 256
MLA_KV_RANK = 128
MLA_WIDTH = MLA_HEADS * MLA_V
ROPE_THETA = 10000.0
D_MIX = SB_WIDTH + SSM_INNER + MLA_WIDTH
D_FF = 2816
FFN_CONV = 3

LANES = 128
SUBLANES = 8
MLA_PAD = 128
HALF = MLA_ROPE // 2
VMEM_LIMIT = 48 * 1024 * 1024
VMEM_LIMIT_FFN = 62 * 1024 * 1024

C_SB = 0
C_Z = C_SB + 3 * SB_WIDTH
C_XBC = C_Z + SSM_INNER
C_CQ = C_XBC + SSM_CONV_DIM
C_CKV = C_CQ + MLA_Q_RANK
C_KPE = C_CKV + MLA_KV_RANK
C_DT = C_KPE + MLA_PAD
N_IN = C_DT + LANES

MLA_Q_SCALE = (MLA_NOPE + MLA_ROPE) ** -0.5 * 1.4426950408889634

SB_Q_SCALE = SB_HEAD_DIM ** -0.5 * 1.4426950408889634
SB_LOG2_FLOOR = -151.0


def _rms(x, g):
    return x * lax.rsqrt(jnp.mean(x * x, axis=-1, keepdims=True) + EPS) * g


def _silu(x):
    return x * (1.0 / (1.0 + jnp.exp(-x)))


def _softplus(x):
    return jnp.maximum(x, 0.0) + jnp.log1p(jnp.exp(-jnp.abs(x)))


def _split3(x):
    a = x.astype(BF16)
    r = x - a.astype(F32)
    b = r.astype(BF16)
    c = (r - b.astype(F32)).astype(BF16)
    return a, b, c


def _lane_tile(x, n):
    return jnp.concatenate([x] * n, axis=1)


def _dot(a, b):
    return jnp.dot(a, b, preferred_element_type=F32)


def _dot_nt(a, b):
    return lax.dot_general(a, b, (((1,), (1,)), ((), ())), preferred_element_type=F32)


def _layer_spec(a, layer):
    rest = a.shape[1:]
    return pl.BlockSpec((None,) + rest, lambda *_: (layer,) + (0,) * len(rest),
                        pipeline_mode=pl.Buffered(1))


def _rope_table_kernel(pos_ref, invf_ref, sign_ref, cos_ref, sin_ref):
    ang = pos_ref[...].astype(F32) * invf_ref[...]
    cos_ref[...] = jnp.cos(ang)
    sin_ref[...] = jnp.sin(ang) * sign_ref[...]


def _rope_tables(pos_col, invf_lane, sign_lane, tm):
    t = pos_col.shape[0]
    return pl.pallas_call(
        _rope_table_kernel,
        grid=(t // tm,),
        in_specs=[pl.BlockSpec((tm, 1), lambda i: (i, 0)),
                  pl.BlockSpec((1, LANES), lambda i: (0, 0)),
                  pl.BlockSpec((1, LANES), lambda i: (0, 0))],
        out_specs=[pl.BlockSpec((tm, LANES), lambda i: (i, 0)),
                   pl.BlockSpec((tm, LANES), lambda i: (i, 0))],
        out_shape=[jax.ShapeDtypeStruct((t, LANES), F32)] * 2,
        compiler_params=pltpu.CompilerParams(dimension_semantics=("parallel",)),
        name="rope_tables",
    )(pos_col, invf_lane, sign_lane)


def _mix_in_kernel(h_ref, g_ref, w_ref, wt_ref, qn_ref, wuq_ref, kvn_ref, wk_ref, wv_ref, cos_ref,
                   sin_ref, sb_ref, z_ref, xbc_ref, dt_ref, qm_ref, km_ref, vm_ref):
    xn = _rms(h_ref[...], g_ref[...]).astype(BF16)

    def proj(lo, hi):
        if lo >= C_CQ:
            return _dot(xn, wt_ref[:, lo - C_CQ:hi - C_CQ])
        return _dot(xn, w_ref[:, lo:hi])

    cos = cos_ref[...]
    sin = sin_ref[...]

    def rope(x):
        return x * cos + pltpu.roll(x, LANES // 2, axis=1) * sin

    cq_raw = proj(C_CQ, C_CKV)
    ckv_raw = proj(C_CKV, C_KPE)
    kpe_raw = proj(C_KPE, C_DT)
    sb = proj(C_SB, C_Z)
    sb_ref[:, :SB_WIDTH] = (sb[:, :SB_WIDTH] * SB_Q_SCALE).astype(BF16)
    sb_ref[:, SB_WIDTH:] = sb[:, SB_WIDTH:].astype(BF16)

    cq = _rms(cq_raw, qn_ref[...]).astype(BF16)
    ckv = _rms(ckv_raw, kvn_ref[...]).astype(BF16)
    qf = _dot(cq, wuq_ref[...])
    kn = _dot(ckv, wk_ref[...])
    lane = lax.broadcasted_iota(jnp.int32, (1, MLA_HEADS * LANES), 1)
    ones = ((lane % LANES < MLA_V) == ((lane // LANES) % 2 == 1)).astype(F32)
    vm_ref[...] = (_dot(ckv, wv_ref[...]) + ones).astype(BF16)

    z_ref[...] = proj(C_Z, C_XBC)
    xbc_ref[...] = proj(C_XBC, C_CQ)
    dt_ref[...] = proj(C_DT, N_IN)

    kpe = rope(kpe_raw)
    for h in range(MLA_HEADS):
        sl = slice(h * MLA_PAD, (h + 1) * MLA_PAD)
        qm_ref[:, sl] = (rope(qf[:, sl]) * MLA_Q_SCALE).astype(BF16)
        km_ref[:, sl] = (kn[:, sl] + kpe).astype(BF16)


def _mix_in(h, params, cos_t, sin_t, layer, tm):
    t = h.shape[0]
    row = lambda n: pl.BlockSpec((tm, n), lambda i: (i, 0))
    out_widths = [(3 * SB_WIDTH, BF16), (SSM_INNER, F32), (SSM_CONV_DIM, F32), (LANES, F32),
                  (MLA_HEADS * MLA_PAD, BF16), (MLA_HEADS * MLA_PAD, BF16),
                  (MLA_HEADS * LANES, BF16)]
    return pl.pallas_call(
        _mix_in_kernel,
        grid=(t // tm,),
        in_specs=[row(D_MODEL)] + [_layer_spec(a, layer) for a in params] + [row(LANES), row(LANES)],
        out_specs=[row(n) for n, _ in out_widths],
        out_shape=[jax.ShapeDtypeStruct((t, n), d) for n, d in out_widths],
        compiler_params=pltpu.CompilerParams(dimension_semantics=("parallel",),
                                             vmem_limit_bytes=VMEM_LIMIT),
        name="mix_in",
    )(h, *params, cos_t, sin_t)


def _sb_kernel(q_ref, k_ref, v_ref, o_ref, c_ref, acc_ref, *, tq):
    tk = tq
    qi = pl.program_id(1)
    lane = lax.broadcasted_iota(jnp.int32, (tq, LANES), 1)
    c_ref[...] = jnp.zeros(c_ref.shape, F32)
    acc_ref[...] = jnp.zeros(acc_ref.shape, F32)

    heads = range(SB_HEADS)
    pair = [slice((h // 2) * LANES, (h // 2 + 1) * LANES) for h in heads]

    def walk(tiles):
        upper = (lax.broadcasted_iota(jnp.int32, (tk, tk), 0) >
                 lax.broadcasted_iota(jnp.int32, (tk, tk), 1)).astype(BF16)
        mask = (lax.broadcasted_iota(jnp.int32, (tq, tk), 1) <
                lax.broadcasted_iota(jnp.int32, (tq, tk), 0))
        ks = [pl.multiple_of(j * tk, tk) for j, _ in tiles]
        qh = []
        for h in heads:
            own = (lane < SB_HEAD_DIM) if h % 2 == 0 else (lane >= SB_HEAD_DIM)
            qh.append(jnp.where(own, q_ref[:, pair[h]], jnp.zeros((), BF16)))
        z = [[_dot_nt(qh[h], k_ref[pl.ds(k0, tk), pair[h]]) for h in heads]
             for k0 in ks]
        log_keep = [[-(jnp.maximum(x, 0.0) + jnp.log2(1.0 + jnp.exp2(-jnp.abs(x)))) for x in zt]
                    for zt in z]
        log_keep = [[jnp.where(mask, x, 0.0) for x in lt] if diag else lt
                    for lt, (_, diag) in zip(log_keep, tiles)]
        tail = []
        for lt in log_keep:
            hi = [x.astype(BF16) for x in lt]
            lo = [(x - y.astype(F32)).astype(BF16) for x, y in zip(lt, hi)]
            tail.append([_dot(hi[h], upper) + _dot(lo[h], upper) for h in heads])
        pv = [None] * SB_HEADS
        c = [c_ref[h] for h in heads]
        for t, (_, diag) in enumerate(tiles):
            for h in heads:
                later = tail[t][h] + _lane_tile(c[h], tk // LANES)
                w = jnp.exp2(z[t][h] + log_keep[t][h] + later)
                if diag:
                    w = jnp.where(mask, w, 0.0)
                p = _dot(w.astype(BF16), v_ref[pl.ds(ks[t], tk), pair[h]])
                pv[h] = p if pv[h] is None else pv[h] + p
                c[h] = c[h] + jnp.sum(log_keep[t][h], axis=1, keepdims=True)
        for h in heads:
            c_ref[h] = c[h]
            acc_ref[h] += pv[h]
        return jnp.max(functools.reduce(jnp.maximum, c))

    def cond(carry):
        j, c_top = carry
        return jnp.logical_and(j >= 0, c_top > SB_LOG2_FLOOR)

    def body(carry):
        j, _ = carry
        return j - 1, walk([(j, False)])

    start = lax.cond(qi > 0,
                     lambda: (qi - 2, walk([(qi, True), (qi - 1, False)])),
                     lambda: (qi - 1, walk([(qi, True)])))
    lax.while_loop(cond, body, start)
    for hp in range(SB_HEADS // 2):
        o_ref[:, hp * LANES:(hp + 1) * LANES] = jnp.where(lane < SB_HEAD_DIM, acc_ref[2 * hp],
                                                          acc_ref[2 * hp + 1])


def _sb_attn(sb, b, s, tq):
    nq = s // tq
    return pl.pallas_call(
        functools.partial(_sb_kernel, tq=tq),
        grid=(b, nq),
        in_specs=[pl.BlockSpec((tq, SB_WIDTH), lambda bi, qi: (bi * nq + qi, 0)),
                  pl.BlockSpec((s, SB_WIDTH), lambda bi, qi: (bi, 1)),
                  pl.BlockSpec((s, SB_WIDTH), lambda bi, qi: (bi, 2))],
        out_specs=pl.BlockSpec((tq, SB_WIDTH), lambda bi, qi: (bi * nq + qi, 0)),
        out_shape=jax.ShapeDtypeStruct((b * s, SB_WIDTH), F32),
        scratch_shapes=[pltpu.VMEM((SB_HEADS, tq, LANES), F32),
                        pltpu.VMEM((SB_HEADS, tq, LANES), F32)],
        compiler_params=pltpu.CompilerParams(dimension_semantics=("parallel", "parallel"),
                                             vmem_limit_bytes=VMEM_LIMIT),
        name="sb_attn",
    )(sb, sb, sb)


def _ssd_kernel(xbc_ref, z_ref, dt_ref, cw_ref, cb_ref, dtb_ref, alog_ref, dl_ref, gn_ref,
                y_ref, tail_ref, xs_ref, state_ref, *, nch):
    l = SSM_CHUNK
    rows = nch * l
    hd = SSM_HEAD_DIM
    pairs = SSM_HEADS // 2

    @pl.when(pl.program_id(1) == 0)
    def _():
        tail_ref[...] = jnp.zeros_like(tail_ref)
        state_ref[...] = jnp.zeros_like(state_ref)

    xs_ref[0:SUBLANES, :] = tail_ref[...]
    xs_ref[SUBLANES:SUBLANES + rows, :] = xbc_ref[...]
    tail_ref[...] = xbc_ref[rows - SUBLANES:rows, :]
    conv = cb_ref[...]
    for k in range(SSM_CONV):
        conv = conv + cw_ref[k:k + 1, :] * xs_ref[pl.ds(SUBLANES - (SSM_CONV - 1) + k, rows), :]
    xa = _silu(conv)
    dt_all = _softplus(dt_ref[...] + dtb_ref[...])
    da_all = dt_all * (-jnp.exp(alog_ref[...]))

    tril = (lax.broadcasted_iota(jnp.int32, (l, l), 0) >=
            lax.broadcasted_iota(jnp.int32, (l, l), 1))
    trilb = tril.astype(BF16)
    lo_half = lax.broadcasted_iota(jnp.int32, (l, LANES), 1) < hd
    lo_half_s = lax.broadcasted_iota(jnp.int32, (SSM_STATE, LANES), 1) < hd
    gn = SSM_GROUPS * SSM_STATE

    for c in range(nch):
        rs = slice(c * l, (c + 1) * l)
        x = xa[rs, :SSM_INNER]
        xb = x.astype(BF16)
        bm = xa[rs, SSM_INNER:SSM_INNER + gn]
        cm = xa[rs, SSM_INNER + gn:]
        d1, d2, d3 = _split3(da_all[rs])
        cs = _dot(trilb, d1) + _dot(trilb, d2) + _dot(trilb, d3)
        cs8 = cs.T[0:SSM_HEADS]
        dt8 = dt_all[rs].T[0:SSM_HEADS]
        bm_t = bm.T
        end8 = cs8[:, l - 1:l]
        w8 = jnp.exp(end8 - cs8) * dt8
        dec8 = jnp.broadcast_to(jnp.exp(end8), (SSM_HEADS, LANES))
        cg = [cm[:, g * SSM_STATE:(g + 1) * SSM_STATE].astype(BF16) for g in range(SSM_GROUPS)]
        bg_t = [bm_t[g * SSM_STATE:(g + 1) * SSM_STATE] for g in range(SSM_GROUPS)]
        cb = [_dot(cg[g], bg_t[g].astype(BF16)) for g in range(SSM_GROUPS)]

        ys = []
        for p in range(pairs):
            g = (2 * p) // (SSM_HEADS // SSM_GROUPS)
            ps = slice(p * LANES, (p + 1) * LANES)
            xp = xb[:, ps]
            y_diag, contrib, col = [], [], []
            for h in (2 * p, 2 * p + 1):
                cs_col = jnp.broadcast_to(cs[:, h:h + 1], (l, l))
                decay = jnp.exp(cs_col - cs8[h:h + 1, :])
                m = jnp.where(tril, cb[g] * decay * dt8[h:h + 1, :], 0.0)
                y_diag.append(_dot(m.astype(BF16), xp))
                contrib.append(_dot((bg_t[g] * w8[h:h + 1, :]).astype(BF16), xp))
                col.append(cs_col)
            st = state_ref[p]
            y_off = _dot(cg[g], st.astype(BF16)) * jnp.exp(jnp.where(lo_half, col[0], col[1]))
            st_decay = jnp.where(lo_half_s, dec8[2 * p:2 * p + 1, :], dec8[2 * p + 1:2 * p + 2, :])
            state_ref[p] = st * st_decay + jnp.where(lo_half_s, contrib[0], contrib[1])
            ys.append(jnp.where(lo_half, y_diag[0], y_diag[1]) + y_off + x[:, ps] * dl_ref[:, ps])
        y = jnp.concatenate(ys, axis=1) * _silu(z_ref[rs, :])
        y_ref[rs, :] = _rms(y, gn_ref[...]).astype(BF16)


def _ssd(xbc, z, dt, params, layer, b, s, nch):
    rows = nch * SSM_CHUNK
    nb = s // rows
    row = lambda n: pl.BlockSpec((rows, n), lambda bi, ci: (bi * nb + ci, 0))
    return pl.pallas_call(
        functools.partial(_ssd_kernel, nch=nch),
        grid=(b, nb),
        in_specs=[row(SSM_CONV_DIM), row(SSM_INNER), row(LANES)]
                 + [_layer_spec(a, layer) for a in params],
        out_specs=row(SSM_INNER),
        out_shape=jax.ShapeDtypeStruct((b * s, SSM_INNER), BF16),
        scratch_shapes=[pltpu.VMEM((SUBLANES, SSM_CONV_DIM), F32),
                        pltpu.VMEM((SUBLANES + rows, SSM_CONV_DIM), F32),
                        pltpu.VMEM((SSM_HEADS // 2, SSM_STATE, LANES), F32)],
        compiler_params=pltpu.CompilerParams(dimension_semantics=("parallel", "arbitrary"),
                                             vmem_limit_bytes=VMEM_LIMIT),
        name="ssd",
    )(xbc, z, dt, *params)


def _mla_kernel(q_ref, k_ref, v_ref, o_ref, m_ref, acc_ref, *, tq, tk):
    qi = pl.program_id(1)
    diag_tiles = tq // tk
    m_ref[...] = jnp.full(m_ref.shape, -1e30, F32)
    acc_ref[...] = jnp.zeros(acc_ref.shape, F32)

    heads = range(MLA_HEADS)

    def step(j, diag):
        ks = pl.multiple_of(j * tk, tk)
        r0 = 0 if diag is None else diag * tk
        rs = slice(r0, tq)
        nr = tq - r0
        sc = []
        for h in heads:
            hs = slice(h * MLA_PAD, (h + 1) * MLA_PAD)
            sc.append(_dot_nt(q_ref[rs, hs], k_ref[pl.ds(ks, tk), hs]))
        if diag is not None:
            keep = (lax.broadcasted_iota(jnp.int32, (nr, tk), 1) <=
                    lax.broadcasted_iota(jnp.int32, (nr, tk), 0))
            sc = [jnp.where(keep, x, -jnp.inf) for x in sc]
        alpha, pv = [], []
        for h in heads:
            m_old = m_ref[h, rs, :]
            m_new = jnp.maximum(m_old, jnp.max(sc[h], axis=1, keepdims=True))
            p = jnp.exp2(sc[h] - _lane_tile(m_new, tk // LANES))
            m_ref[h, rs, :] = m_new
            alpha.append(jnp.exp2(m_old - m_new))
            pv.append(_dot(p.astype(BF16), v_ref[pl.ds(ks, tk), h * LANES:(h + 1) * LANES]))
        for h in heads:
            acc_ref[h, rs, :] = alpha[h] * acc_ref[h, rs, :] + pv[h]

    def body(j, carry):
        step(j, None)
        return carry

    lax.fori_loop(0, qi * diag_tiles, body, 0)
    for u in range(diag_tiles):
        step(qi * diag_tiles + u, u)
    lane = lax.broadcasted_iota(jnp.int32, (tq, LANES), 1)
    for hp in range(MLA_HEADS // 2):
        even = acc_ref[2 * hp]
        odd = acc_ref[2 * hp + 1]
        even = even / pltpu.roll(even, LANES // 2, axis=1)
        odd = odd / pltpu.roll(odd, LANES // 2, axis=1)
        o_ref[:, hp * LANES:(hp + 1) * LANES] = jnp.where(lane < MLA_V, even, odd)


def _mla_attn(qm, km, vm, b, s, tq, tk):
    nq = s // tq
    return pl.pallas_call(
        functools.partial(_mla_kernel, tq=tq, tk=tk),
        grid=(b, nq),
        in_specs=[pl.BlockSpec((tq, MLA_HEADS * MLA_PAD), lambda bi, qi: (bi * nq + qi, 0)),
                  pl.BlockSpec((s, MLA_HEADS * MLA_PAD), lambda bi, qi: (bi, 0),
                               pipeline_mode=pl.Buffered(1)),
                  pl.BlockSpec((s, MLA_HEADS * LANES), lambda bi, qi: (bi, 0),
                               pipeline_mode=pl.Buffered(1))],
        out_specs=pl.BlockSpec((tq, MLA_WIDTH), lambda bi, qi: (bi * nq + qi, 0)),
        out_shape=jax.ShapeDtypeStruct((b * s, MLA_WIDTH), F32),
        scratch_shapes=[pltpu.VMEM((MLA_HEADS, tq, LANES), F32),
                        pltpu.VMEM((MLA_HEADS, tq, LANES), F32)],
        compiler_params=pltpu.CompilerParams(dimension_semantics=("parallel", "parallel"),
                                             vmem_limit_bytes=VMEM_LIMIT),
        name="mla_attn",
    )(qm, km, vm)


def _out_ffn_kernel(h_ref, ya_ref, yb_ref, yc_ref, ga_ref, gc_ref, wo_ref, g_ref, wup_ref, cw_ref,
                    cb_ref, wd_ref, fin_ref,
                    o_ref, hn_ref, acc_ref, carry_g, carry_v, sg0, sg1, sv0, sv1, act0, act1, *, tm,
                    tiles_per_seq, final_norm):
    c1 = SB_WIDTH
    c2 = SB_WIDTH + SSM_INNER
    h1 = h_ref[...] + _dot(yb_ref[...], wo_ref[c1:c2, :])
    ya = _rms(ya_ref[...], ga_ref[...]).astype(BF16)
    yc = _rms(yc_ref[...], gc_ref[...]).astype(BF16)
    h1 = h1 + _dot(ya, wo_ref[:c1, :]) + _dot(yc, wo_ref[c2:, :])
    hn_ref[...] = _rms(h1, g_ref[...]).astype(BF16)
    acc_ref[...] = h1

    first = (pl.program_id(0) % tiles_per_seq) == 0
    pad = SUBLANES
    tf = sg0.shape[1]
    nj = D_FF // tf

    def cols(half, j):
        return pl.ds(pl.multiple_of(half * D_FF + j * tf, tf), tf)

    sg, sv, act = (sg0, sg1), (sv0, sv1), (act0, act1)

    def up_proj(j, slot):
        for half, carry_ref, s_ref in ((0, carry_g, sg[slot]), (1, carry_v, sv[slot])):
            up = _dot(hn_ref[...], wup_ref[:, cols(half, j)])
            s_ref[0:pad, :] = jnp.where(first, 0.0, carry_ref[j])
            s_ref[pad:pad + tm, :] = up
            carry_ref[j] = up[tm - pad:tm, :]

    def conv_act(j, slot):
        def conv(half, s_ref):
            cw = cw_ref[:, cols(half, j)]
            return (cb_ref[:, cols(half, j)] + cw[0:1, :] * s_ref[pl.ds(pad - 2, tm), :]
                    + cw[1:2, :] * s_ref[pl.ds(pad - 1, tm), :]
                    + cw[2:3, :] * s_ref[pl.ds(pad, tm), :])

        act[slot][...] = (_silu(conv(0, sg[slot])) * conv(1, sv[slot])).astype(BF16)

    def down_proj(j, slot):
        acc_ref[...] += _dot(act[slot][...], wd_ref[pl.ds(pl.multiple_of(j * tf, tf), tf), :])

    def steady(j, slot):
        up_proj(j + 1, 1 - slot)
        conv_act(j, slot)
        down_proj(j, slot)

    up_proj(0, 0)
    trips = (nj - 1) // FFN_UNROLL

    def group(p, carry):
        for u in range(FFN_UNROLL):
            steady(FFN_UNROLL * p + u, u % 2)
        return carry

    lax.fori_loop(0, trips, group, 0)
    for j in range(trips * FFN_UNROLL, nj - 1):
        steady(j, j % 2)
    conv_act(nj - 1, (nj - 1) % 2)
    down_proj(nj - 1, (nj - 1) % 2)
    out = acc_ref[...]
    if final_norm:
        out = _rms(out, fin_ref[...])
    o_ref[...] = out


def _out_ffn(h, ya, yb, yc, params, fin, layer, s, tm, tf, final_norm):
    t = h.shape[0]
    nj = D_FF // tf
    row = lambda n: pl.BlockSpec((tm, n), lambda i: (i, 0))
    return pl.pallas_call(
        functools.partial(_out_ffn_kernel, tm=tm, tiles_per_seq=s // tm, final_norm=final_norm),
        grid=(t // tm,),
        in_specs=[row(D_MODEL), row(SB_WIDTH), row(SSM_INNER), row(MLA_WIDTH)]
                 + [_layer_spec(a, layer) for a in params]
                 + [pl.BlockSpec(fin.shape, lambda i: (0, 0), pipeline_mode=pl.Buffered(1))],
        out_specs=row(D_MODEL),
        out_shape=jax.ShapeDtypeStruct((t, D_MODEL), F32),
        scratch_shapes=[pltpu.VMEM((tm, D_MODEL), BF16), pltpu.VMEM((tm, D_MODEL), F32),
                        pltpu.VMEM((nj, SUBLANES, tf), F32), pltpu.VMEM((nj, SUBLANES, tf), F32),
                        ] + [pltpu.VMEM((SUBLANES + tm, tf), F32)] * 4 + [pltpu.VMEM((tm, tf), BF16)] * 2,
        compiler_params=pltpu.CompilerParams(
            dimension_semantics=("arbitrary",), vmem_limit_bytes=VMEM_LIMIT_FFN),
        name="out_ffn",
    )(h, ya, yb, yc, *params, fin)


def _zeros_like_cols(a, n):
    return jnp.zeros(a.shape[:-1] + (n,), a.dtype)


def _pack_in_proj(w):
    o_dt = C_CQ
    o_cq = o_dt + SSM_HEADS
    o_kr = o_cq + MLA_Q_RANK + MLA_KV_RANK
    kr = w[..., o_kr:o_kr + MLA_ROPE]
    z16, z32 = _zeros_like_cols(w, 16), _zeros_like_cols(w, 32)
    tail = jnp.concatenate([w[..., o_cq:o_kr], z32, kr[..., :HALF], z16, z32, kr[..., HALF:], z16,
                            w[..., o_dt:o_cq], _zeros_like_cols(w, LANES - SSM_HEADS)], axis=-1)
    return w[..., :o_dt].astype(BF16), tail.astype(BF16)


def _pack_uq(w):
    l, r, _ = w.shape
    w = w.reshape(l, r, MLA_HEADS, MLA_NOPE + MLA_ROPE)
    nope, rp = w[..., :MLA_NOPE], w[..., MLA_NOPE:]
    z16 = _zeros_like_cols(w, 16)
    out = jnp.concatenate([nope[..., :32], rp[..., :HALF], z16, nope[..., 32:], rp[..., HALF:], z16],
                          axis=-1)
    return out.reshape(l, r, MLA_HEADS * MLA_PAD).astype(BF16)


def _pack_ukv(w):
    l, r, _ = w.shape
    w = w.reshape(l, r, MLA_HEADS, MLA_NOPE + MLA_V)
    nope, wv = w[..., :MLA_NOPE], w[..., MLA_NOPE:]
    z32, z64 = _zeros_like_cols(w, 32), _zeros_like_cols(w, LANES - MLA_V)
    wk = jnp.concatenate([nope[..., :32], z32, nope[..., 32:], z32], axis=-1)
    odd = (jnp.arange(MLA_HEADS) % 2 == 1)[None, None, :, None]
    wv = jnp.where(odd, jnp.concatenate([z64, wv], axis=-1), jnp.concatenate([wv, z64], axis=-1))
    flat = lambda a: a.reshape(l, r, MLA_HEADS * LANES).astype(BF16)
    return flat(wk), flat(wv)


def _row(v, width=None):
    if width is not None and width > v.shape[-1]:
        v = jnp.concatenate([v, _zeros_like_cols(v, width - v.shape[-1])], axis=-1)
    return v[:, None, :]


TM_ROPE = 1024
TM_IN = 1024
TQ_SB = 256
TQ_MLA = 1024
TK_MLA = 512
SSD_BLOCK_CHUNKS = 4
TM_FFN = 1024
TF_FFN = 256
FFN_UNROLL = 2


def kernel(x, positions, mix_norm, w_in, sb_out_norm, ssm_conv_w, ssm_conv_b, ssm_dt_bias, ssm_a_log, ssm_d, ssm_out_norm, mla_q_norm, mla_w_uq, mla_kv_norm, mla_w_ukv, mla_out_norm, w_out, ffn_norm, ffn_w_up, ffn_conv_w, ffn_conv_b, ffn_w_down, final_norm):
    b, s, d = x.shape
    t = b * s
    depth = w_in.shape[0]
    h = x.reshape(t, d)

    inv_freq = 1.0 / (ROPE_THETA ** (jnp.arange(0, MLA_ROPE, 2, dtype=F32) / MLA_ROPE))
    z16, z32 = jnp.zeros((16,), F32), jnp.zeros((32,), F32)
    invf_lane = jnp.concatenate([z32, inv_freq, z16, z32, inv_freq, z16])[None, :]
    sign_lane = jnp.concatenate([z32, -jnp.ones((16,), F32), z16, z32, jnp.ones((16,), F32), z16])[None, :]
    cos_t, sin_t = _rope_tables(positions.reshape(t, 1), invf_lane, sign_lane, TM_ROPE)

    w_main, w_tail = _pack_in_proj(w_in)
    w_uk, w_uv = _pack_ukv(mla_w_ukv)
    in_params = (_row(mix_norm), w_main, w_tail, _row(mla_q_norm), _pack_uq(mla_w_uq),
                 _row(mla_kv_norm), w_uk, w_uv)
    ssd_params = (ssm_conv_w, _row(ssm_conv_b), _row(ssm_dt_bias, LANES), _row(ssm_a_log, LANES),
                  _row(jnp.repeat(ssm_d, SSM_HEAD_DIM, axis=-1)), _row(ssm_out_norm))
    out_params = (_row(sb_out_norm), _row(mla_out_norm), w_out.astype(BF16), _row(ffn_norm),
                  ffn_w_up.astype(BF16), ffn_conv_w, _row(ffn_conv_b), ffn_w_down.astype(BF16))

    for i in range(depth):
        sb, z, xbc, dt, qm, km, vm = _mix_in(h, in_params, cos_t, sin_t, i, TM_IN)
        ya = _sb_attn(sb, b, s, TQ_SB)
        yb = _ssd(xbc, z, dt, ssd_params, i, b, s, SSD_BLOCK_CHUNKS)
        yc = _mla_attn(qm, km, vm, b, s, TQ_MLA, TK_MLA)
        h = _out_ffn(h, ya, yb, yc, out_params, final_norm[None, :], i, s, TM_FFN, TF_FFN,
                     final_norm=(i == depth - 1))
    return h.reshape(b, s, d)
```

```python
import functools

import jax
import jax.numpy as jnp
from jax import lax
from jax.experimental import pallas as pl
from jax.experimental.pallas import tpu as pltpu

F32 = jnp.float32
BF16 = jnp.bfloat16

D_MODEL = 1024
EPS = 1e-6
SB_HEADS = 4
SB_HEAD_DIM = 64
SB_WIDTH = SB_HEADS * SB_HEAD_DIM
SSM_HEADS = 8
SSM_HEAD_DIM = 64
SSM_INNER = SSM_HEADS * SSM_HEAD_DIM
SSM_GROUPS = 2
SSM_STATE = 64
SSM_CONV = 4
SSM_CHUNK = 128
SSM_CONV_DIM = SSM_INNER + 2 * SSM_GROUPS * SSM_STATE
MLA_HEADS = 4
MLA_NOPE = 64
MLA_ROPE = 32
MLA_V = 64
MLA_Q_RANK = 256
MLA_KV_RANK = 128
MLA_WIDTH = MLA_HEADS * MLA_V
ROPE_THETA = 10000.0
D_MIX = SB_WIDTH + SSM_INNER + MLA_WIDTH
D_FF = 2816
FFN_CONV = 3

LANES = 128
SUBLANES = 8
MLA_PAD = 128
HALF = MLA_ROPE // 2
VMEM_LIMIT = 48 * 1024 * 1024
VMEM_LIMIT_FFN = 62 * 1024 * 1024

C_SB = 0
C_Z = C_SB + 3 * SB_WIDTH
C_XBC = C_Z + SSM_INNER
C_CQ = C_XBC + SSM_CONV_DIM
C_CKV = C_CQ + MLA_Q_RANK
C_KPE = C_CKV + MLA_KV_RANK
C_DT = C_KPE + MLA_PAD
N_IN = C_DT + LANES

MLA_Q_SCALE = (MLA_NOPE + MLA_ROPE) ** -0.5 * 1.4426950408889634

SB_Q_SCALE = SB_HEAD_DIM ** -0.5 * 1.4426950408889634
SB_LOG2_FLOOR = -151.0


def _rms(x, g):
    return x * lax.rsqrt(jnp.mean(x * x, axis=-1, keepdims=True) + EPS) * g


def _silu(x):
    return x * (1.0 / (1.0 + jnp.exp(-x)))


def _softplus(x):
    return jnp.maximum(x, 0.0) + jnp.log1p(jnp.exp(-jnp.abs(x)))


def _split3(x):
    a = x.astype(BF16)
    r = x - a.astype(F32)
    b = r.astype(BF16)
    c = (r - b.astype(F32)).astype(BF16)
    return a, b, c


def _lane_tile(x, n):
    return jnp.concatenate([x] * n, axis=1)


def _dot(a, b):
    return jnp.dot(a, b, preferred_element_type=F32)


def _dot_nt(a, b):
    return lax.dot_general(a, b, (((1,), (1,)), ((), ())), preferred_element_type=F32)


def _layer_spec(a, layer):
    rest = a.shape[1:]
    return pl.BlockSpec((None,) + rest, lambda *_: (layer,) + (0,) * len(rest),
                        pipeline_mode=pl.Buffered(1))


def _mix_in_kernel(h_ref, g_ref, w_ref, wt_ref, qn_ref, wuq_ref, kvn_ref, wk_ref, wv_ref, *rest,
                   make_tables):
    if make_tables:
        (pos_ref, invf_ref, sign_ref, sb_ref, z_ref, xbc_ref, dt_ref, qm_ref, km_ref, vm_ref,
         cos_out, sin_out) = rest
        ang = pos_ref[...].astype(F32) * invf_ref[...]
        cos = jnp.cos(ang)
        sin = jnp.sin(ang) * sign_ref[...]
        cos_out[...] = cos
        sin_out[...] = sin
    else:
        cos_ref, sin_ref, sb_ref, z_ref, xbc_ref, dt_ref, qm_ref, km_ref, vm_ref = rest
        cos = cos_ref[...]
        sin = sin_ref[...]
    xn = _rms(h_ref[...], g_ref[...]).astype(BF16)

    def proj(lo, hi):
        if lo >= C_CQ:
            return _dot(xn, wt_ref[:, lo - C_CQ:hi - C_CQ])
        return _dot(xn, w_ref[:, lo:hi])

    def rope(x):
        return x * cos + pltpu.roll(x, LANES // 2, axis=1) * sin

    cq_raw = proj(C_CQ, C_CKV)
    ckv_raw = proj(C_CKV, C_KPE)
    kpe_raw = proj(C_KPE, C_DT)
    sb = proj(C_SB, C_Z)
    sb_ref[:, :SB_WIDTH] = (sb[:, :SB_WIDTH] * SB_Q_SCALE).astype(BF16)
    sb_ref[:, SB_WIDTH:] = sb[:, SB_WIDTH:].astype(BF16)

    cq = _rms(cq_raw, qn_ref[...]).astype(BF16)
    ckv = _rms(ckv_raw, kvn_ref[...]).astype(BF16)
    qf = _dot(cq, wuq_ref[...])
    kn = _dot(ckv, wk_ref[...])
    lane = lax.broadcasted_iota(jnp.int32, (1, MLA_HEADS * LANES), 1)
    ones = ((lane % LANES < MLA_V) == ((lane // LANES) % 2 == 1)).astype(F32)
    vm_ref[...] = (_dot(ckv, wv_ref[...]) + ones).astype(BF16)

    z_ref[...] = proj(C_Z, C_XBC)
    xbc_ref[...] = proj(C_XBC, C_CQ)
    dt_ref[...] = proj(C_DT, N_IN)

    kpe = rope(kpe_raw)
    for h in range(MLA_HEADS):
        sl = slice(h * MLA_PAD, (h + 1) * MLA_PAD)
        qm_ref[:, sl] = (rope(qf[:, sl]) * MLA_Q_SCALE).astype(BF16)
        km_ref[:, sl] = (kn[:, sl] + kpe).astype(BF16)


def _mix_in(h, params, rope_in, layer, tm):
    t = h.shape[0]
    make_tables = len(rope_in) == 3
    row = lambda n: pl.BlockSpec((tm, n), lambda i: (i, 0))
    lane_row = pl.BlockSpec((1, LANES), lambda i: (0, 0))
    out_widths = [(3 * SB_WIDTH, BF16), (SSM_INNER, F32), (SSM_CONV_DIM, F32), (LANES, F32),
                  (MLA_HEADS * MLA_PAD, BF16), (MLA_HEADS * MLA_PAD, BF16),
                  (MLA_HEADS * LANES, BF16)]
    if make_tables:
        rope_specs = [row(1), lane_row, lane_row]
        out_widths += [(LANES, F32), (LANES, F32)]
    else:
        rope_specs = [row(LANES), row(LANES)]
    param_specs = [_layer_spec(a, layer) for a in params]
    param_specs[1] = pl.BlockSpec((None, D_MODEL, C_CQ), lambda *_: (layer, 0, 0),
                                  pipeline_mode=pl.Buffered(1))
    return pl.pallas_call(
        functools.partial(_mix_in_kernel, make_tables=make_tables),
        grid=(t // tm,),
        in_specs=[row(D_MODEL)] + param_specs + rope_specs,
        out_specs=[row(n) for n, _ in out_widths],
        out_shape=[jax.ShapeDtypeStruct((t, n), d) for n, d in out_widths],
        compiler_params=pltpu.CompilerParams(dimension_semantics=("parallel",),
                                             vmem_limit_bytes=VMEM_LIMIT),
        name="mix_in",
    )(h, *params, *rope_in)


def _sb_kernel(q_ref, k_ref, v_ref, o_ref, c_ref, acc_ref, *, tq):
    tk = tq
    qi = pl.program_id(1)
    lane = lax.broadcasted_iota(jnp.int32, (tq, LANES), 1)
    c_ref[...] = jnp.zeros(c_ref.shape, F32)
    acc_ref[...] = jnp.zeros(acc_ref.shape, F32)

    heads = range(SB_HEADS)
    pair = [slice((h // 2) * LANES, (h // 2 + 1) * LANES) for h in heads]

    def walk(tiles):
        upper = (lax.broadcasted_iota(jnp.int32, (tk, tk), 0) >
                 lax.broadcasted_iota(jnp.int32, (tk, tk), 1)).astype(BF16)
        mask = (lax.broadcasted_iota(jnp.int32, (tq, tk), 1) <
                lax.broadcasted_iota(jnp.int32, (tq, tk), 0))
        ks = [pl.multiple_of(j * tk, tk) for j, _ in tiles]
        qh = []
        for h in heads:
            own = (lane < SB_HEAD_DIM) if h % 2 == 0 else (lane >= SB_HEAD_DIM)
            qh.append(jnp.where(own, q_ref[:, pair[h]], jnp.zeros((), BF16)))
        z = [[_dot_nt(qh[h], k_ref[pl.ds(k0, tk), pair[h]]) for h in heads]
             for k0 in ks]
        log_keep = [[-(jnp.maximum(x, 0.0) + jnp.log2(1.0 + jnp.exp2(-jnp.abs(x)))) for x in zt]
                    for zt in z]
        log_keep = [[jnp.where(mask, x, 0.0) for x in lt] if diag else lt
                    for lt, (_, diag) in zip(log_keep, tiles)]
        tail = []
        for lt in log_keep:
            hi = [x.astype(BF16) for x in lt]
            lo = [(x - y.astype(F32)).astype(BF16) for x, y in zip(lt, hi)]
            tail.append([_dot(hi[h], upper) + _dot(lo[h], upper) for h in heads])
        pv = [None] * SB_HEADS
        c = [c_ref[h] for h in heads]
        for t, (_, diag) in enumerate(tiles):
            for h in heads:
                later = tail[t][h] + _lane_tile(c[h], tk // LANES)
                w = jnp.exp2(z[t][h] + log_keep[t][h] + later)
                if diag:
                    w = jnp.where(mask, w, 0.0)
                p = _dot(w.astype(BF16), v_ref[pl.ds(ks[t], tk), pair[h]])
                pv[h] = p if pv[h] is None else pv[h] + p
                c[h] = c[h] + jnp.sum(log_keep[t][h], axis=1, keepdims=True)
        for h in heads:
            c_ref[h] = c[h]
            acc_ref[h] += pv[h]
        return jnp.max(functools.reduce(jnp.maximum, c))

    def cond(carry):
        j, c_top = carry
        return jnp.logical_and(j >= 0, c_top > SB_LOG2_FLOOR)

    def body(carry):
        j, _ = carry
        return j - 1, walk([(j, False)])

    start = lax.cond(qi > 0,
                     lambda: (qi - 2, walk([(qi, True), (qi - 1, False)])),
                     lambda: (qi - 1, walk([(qi, True)])))
    lax.while_loop(cond, body, start)
    for hp in range(SB_HEADS // 2):
        o_ref[:, hp * LANES:(hp + 1) * LANES] = jnp.where(lane < SB_HEAD_DIM, acc_ref[2 * hp],
                                                          acc_ref[2 * hp + 1])


def _sb_attn(sb, b, s, tq):
    nq = s // tq
    return pl.pallas_call(
        functools.partial(_sb_kernel, tq=tq),
        grid=(b, nq),
        in_specs=[pl.BlockSpec((tq, SB_WIDTH), lambda bi, qi: (bi * nq + qi, 0)),
                  pl.BlockSpec((s, SB_WIDTH), lambda bi, qi: (bi, 1)),
                  pl.BlockSpec((s, SB_WIDTH), lambda bi, qi: (bi, 2))],
        out_specs=pl.BlockSpec((tq, SB_WIDTH), lambda bi, qi: (bi * nq + qi, 0)),
        out_shape=jax.ShapeDtypeStruct((b * s, SB_WIDTH), F32),
        scratch_shapes=[pltpu.VMEM((SB_HEADS, tq, LANES), F32),
                        pltpu.VMEM((SB_HEADS, tq, LANES), F32)],
        compiler_params=pltpu.CompilerParams(dimension_semantics=("parallel", "parallel"),
                                             vmem_limit_bytes=VMEM_LIMIT),
        name="sb_attn",
    )(sb, sb, sb)


def _ssd_kernel(xbc_ref, z_ref, dt_ref, cw_ref, cb_ref, dtb_ref, alog_ref, dl_ref, gn_ref,
                y_ref, tail_ref, xs_ref, state_ref, *, nch):
    l = SSM_CHUNK
    rows = nch * l
    hd = SSM_HEAD_DIM
    pairs = SSM_HEADS // 2

    @pl.when(pl.program_id(1) == 0)
    def _():
        tail_ref[...] = jnp.zeros_like(tail_ref)
        state_ref[...] = jnp.zeros_like(state_ref)

    xs_ref[0:SUBLANES, :] = tail_ref[...]
    xs_ref[SUBLANES:SUBLANES + rows, :] = xbc_ref[...]
    tail_ref[...] = xbc_ref[rows - SUBLANES:rows, :]
    conv = cb_ref[...]
    for k in range(SSM_CONV):
        conv = conv + cw_ref[k:k + 1, :] * xs_ref[pl.ds(SUBLANES - (SSM_CONV - 1) + k, rows), :]
    xa = _silu(conv)
    dt_all = _softplus(dt_ref[...] + dtb_ref[...])
    da_all = dt_all * (-jnp.exp(alog_ref[...]))

    tril = (lax.broadcasted_iota(jnp.int32, (l, l), 0) >=
            lax.broadcasted_iota(jnp.int32, (l, l), 1))
    trilb = tril.astype(BF16)
    lo_half = lax.broadcasted_iota(jnp.int32, (l, LANES), 1) < hd
    lo_half_s = lax.broadcasted_iota(jnp.int32, (SSM_STATE, LANES), 1) < hd
    gn = SSM_GROUPS * SSM_STATE

    for c in range(nch):
        rs = slice(c * l, (c + 1) * l)
        x = xa[rs, :SSM_INNER]
        xb = x.astype(BF16)
        bm = xa[rs, SSM_INNER:SSM_INNER + gn]
        cm = xa[rs, SSM_INNER + gn:]
        d1, d2, d3 = _split3(da_all[rs])
        cs = _dot(trilb, d1) + _dot(trilb, d2) + _dot(trilb, d3)
        cs8 = cs.T[0:SSM_HEADS]
        dt8 = dt_all[rs].T[0:SSM_HEADS]
        bm_t = bm.T
        end8 = cs8[:, l - 1:l]
        w8 = jnp.exp(end8 - cs8) * dt8
        dec8 = jnp.broadcast_to(jnp.exp(end8), (SSM_HEADS, LANES))
        cg = [cm[:, g * SSM_STATE:(g + 1) * SSM_STATE].astype(BF16) for g in range(SSM_GROUPS)]
        bg_t = [bm_t[g * SSM_STATE:(g + 1) * SSM_STATE] for g in range(SSM_GROUPS)]
        cb = [_dot(cg[g], bg_t[g].astype(BF16)) for g in range(SSM_GROUPS)]

        ys = []
        for p in range(pairs):
            g = (2 * p) // (SSM_HEADS // SSM_GROUPS)
            ps = slice(p * LANES, (p + 1) * LANES)
            xp = xb[:, ps]
            y_diag, contrib, col = [], [], []
            for h in (2 * p, 2 * p + 1):
                cs_col = jnp.broadcast_to(cs[:, h:h + 1], (l, l))
                decay = jnp.exp(cs_col - cs8[h:h + 1, :])
                m = jnp.where(tril, cb[g] * decay * dt8[h:h + 1, :], 0.0)
                y_diag.append(_dot(m.astype(BF16), xp))
                contrib.append(_dot((bg_t[g] * w8[h:h + 1, :]).astype(BF16), xp))
                col.append(cs_col)
            st = state_ref[p]
            y_off = _dot(cg[g], st.astype(BF16)) * jnp.exp(jnp.where(lo_half, col[0], col[1]))
            st_decay = jnp.where(lo_half_s, dec8[2 * p:2 * p + 1, :], dec8[2 * p + 1:2 * p + 2, :])
            state_ref[p] = st * st_decay + jnp.where(lo_half_s, contrib[0], contrib[1])
            ys.append(jnp.where(lo_half, y_diag[0], y_diag[1]) + y_off + x[:, ps] * dl_ref[:, ps])
        y = jnp.concatenate(ys, axis=1) * _silu(z_ref[rs, :])
        y_ref[rs, :] = _rms(y, gn_ref[...]).astype(BF16)


def _ssd(xbc, z, dt, params, layer, b, s, nch):
    rows = nch * SSM_CHUNK
    nb = s // rows
    row = lambda n: pl.BlockSpec((rows, n), lambda bi, ci: (bi * nb + ci, 0))
    return pl.pallas_call(
        functools.partial(_ssd_kernel, nch=nch),
        grid=(b, nb),
        in_specs=[row(SSM_CONV_DIM), row(SSM_INNER), row(LANES)]
                 + [_layer_spec(a, layer) for a in params],
        out_specs=row(SSM_INNER),
        out_shape=jax.ShapeDtypeStruct((b * s, SSM_INNER), BF16),
        scratch_shapes=[pltpu.VMEM((SUBLANES, SSM_CONV_DIM), F32),
                        pltpu.VMEM((SUBLANES + rows, SSM_CONV_DIM), F32),
                        pltpu.VMEM((SSM_HEADS // 2, SSM_STATE, LANES), F32)],
        compiler_params=pltpu.CompilerParams(dimension_semantics=("parallel", "arbitrary"),
                                             vmem_limit_bytes=VMEM_LIMIT),
        name="ssd",
    )(xbc, z, dt, *params)


def _mla_kernel(q_ref, k_ref, v_ref, o_ref, m_ref, acc_ref, *, tq, tk):
    qi = pl.program_id(1)
    diag_tiles = tq // tk
    m_ref[...] = jnp.full(m_ref.shape, -1e30, F32)
    acc_ref[...] = jnp.zeros(acc_ref.shape, F32)

    heads = range(MLA_HEADS)

    def step(j, diag):
        ks = pl.multiple_of(j * tk, tk)
        r0 = 0 if diag is None else diag * tk
        rs = slice(r0, tq)
        nr = tq - r0
        sc = []
        for h in heads:
            hs = slice(h * MLA_PAD, (h + 1) * MLA_PAD)
            sc.append(_dot_nt(q_ref[rs, hs], k_ref[pl.ds(ks, tk), hs]))
        if diag is not None:
            keep = (lax.broadcasted_iota(jnp.int32, (nr, tk), 1) <=
                    lax.broadcasted_iota(jnp.int32, (nr, tk), 0))
            sc = [jnp.where(keep, x, -jnp.inf) for x in sc]
        alpha, pv = [], []
        for h in heads:
            m_old = m_ref[h, rs, :]
            m_new = jnp.maximum(m_old, jnp.max(sc[h], axis=1, keepdims=True))
            p = jnp.exp2(sc[h] - _lane_tile(m_new, tk // LANES))
            m_ref[h, rs, :] = m_new
            alpha.append(jnp.exp2(m_old - m_new))
            pv.append(_dot(p.astype(BF16), v_ref[pl.ds(ks, tk), h * LANES:(h + 1) * LANES]))
        for h in heads:
            acc_ref[h, rs, :] = alpha[h] * acc_ref[h, rs, :] + pv[h]

    def body(j, carry):
        step(j, None)
        return carry

    lax.fori_loop(0, qi * diag_tiles, body, 0)
    for u in range(diag_tiles):
        step(qi * diag_tiles + u, u)
    lane = lax.broadcasted_iota(jnp.int32, (tq, LANES), 1)
    for hp in range(MLA_HEADS // 2):
        even = acc_ref[2 * hp]
        odd = acc_ref[2 * hp + 1]
        even = even / pltpu.roll(even, LANES // 2, axis=1)
        odd = odd / pltpu.roll(odd, LANES // 2, axis=1)
        o_ref[:, hp * LANES:(hp + 1) * LANES] = jnp.where(lane < MLA_V, even, odd)


def _mla_attn(qm, km, vm, b, s, tq, tk):
    nq = s // tq
    return pl.pallas_call(
        functools.partial(_mla_kernel, tq=tq, tk=tk),
        grid=(b, nq),
        in_specs=[pl.BlockSpec((tq, MLA_HEADS * MLA_PAD), lambda bi, qi: (bi * nq + qi, 0)),
                  pl.BlockSpec((s, MLA_HEADS * MLA_PAD), lambda bi, qi: (bi, 0),
                               pipeline_mode=pl.Buffered(1)),
                  pl.BlockSpec((s, MLA_HEADS * LANES), lambda bi, qi: (bi, 0),
                               pipeline_mode=pl.Buffered(1))],
        out_specs=pl.BlockSpec((tq, MLA_WIDTH), lambda bi, qi: (bi * nq + qi, 0)),
        out_shape=jax.ShapeDtypeStruct((b * s, MLA_WIDTH), F32),
        scratch_shapes=[pltpu.VMEM((MLA_HEADS, tq, LANES), F32),
                        pltpu.VMEM((MLA_HEADS, tq, LANES), F32)],
        compiler_params=pltpu.CompilerParams(dimension_semantics=("parallel", "parallel"),
                                             vmem_limit_bytes=VMEM_LIMIT),
        name="mla_attn",
    )(qm, km, vm)


def _out_ffn_kernel(h_ref, ya_ref, yb_ref, yc_ref, ga_ref, gc_ref, wo_ref, g_ref, wup_ref, cw_ref,
                    cb_ref, wd_ref, fin_ref,
                    o_ref, hn_ref, acc_ref, carry_g, carry_v, sg0, sg1, sv0, sv1, act0, act1, *, tm,
                    tiles_per_seq, final_norm):
    c1 = SB_WIDTH
    c2 = SB_WIDTH + SSM_INNER
    h1 = h_ref[...] + _dot(yb_ref[...], wo_ref[c1:c2, :])
    ya = _rms(ya_ref[...], ga_ref[...]).astype(BF16)
    yc = _rms(yc_ref[...], gc_ref[...]).astype(BF16)
    h1 = h1 + _dot(ya, wo_ref[:c1, :]) + _dot(yc, wo_ref[c2:, :])
    hn_ref[...] = _rms(h1, g_ref[...]).astype(BF16)
    acc_ref[...] = h1

    first = (pl.program_id(0) % tiles_per_seq) == 0
    pad = SUBLANES
    tf = sg0.shape[1]
    nj = D_FF // tf

    def cols(half, j):
        return pl.ds(pl.multiple_of(half * D_FF + j * tf, tf), tf)

    sg, sv, act = (sg0, sg1), (sv0, sv1), (act0, act1)

    def up_proj(j, slot):
        for half, carry_ref, s_ref in ((0, carry_g, sg[slot]), (1, carry_v, sv[slot])):
            up = _dot(hn_ref[...], wup_ref[:, cols(half, j)])
            s_ref[0:pad, :] = jnp.where(first, 0.0, carry_ref[j])
            s_ref[pad:pad + tm, :] = up
            carry_ref[j] = up[tm - pad:tm, :]

    def conv_act(j, slot):
        def conv(half, s_ref):
            cw = cw_ref[:, cols(half, j)]
            return (cb_ref[:, cols(half, j)] + cw[0:1, :] * s_ref[pl.ds(pad - 2, tm), :]
                    + cw[1:2, :] * s_ref[pl.ds(pad - 1, tm), :]
                    + cw[2:3, :] * s_ref[pl.ds(pad, tm), :])

        act[slot][...] = (_silu(conv(0, sg[slot])) * conv(1, sv[slot])).astype(BF16)

    def down_proj(j, slot):
        acc_ref[...] += _dot(act[slot][...], wd_ref[pl.ds(pl.multiple_of(j * tf, tf), tf), :])

    def steady(j, slot):
        up_proj(j + 1, 1 - slot)
        conv_act(j, slot)
        down_proj(j, slot)

    up_proj(0, 0)
    trips = (nj - 1) // FFN_UNROLL

    def group(p, carry):
        for u in range(FFN_UNROLL):
            steady(FFN_UNROLL * p + u, u % 2)
        return carry

    lax.fori_loop(0, trips, group, 0)
    for j in range(trips * FFN_UNROLL, nj - 1):
        steady(j, j % 2)
    conv_act(nj - 1, (nj - 1) % 2)
    down_proj(nj - 1, (nj - 1) % 2)
    out = acc_ref[...]
    if final_norm:
        out = _rms(out, fin_ref[...])
    o_ref[...] = out


def _out_ffn(h, ya, yb, yc, params, fin, layer, s, tm, tf, final_norm):
    t = h.shape[0]
    nj = D_FF // tf
    row = lambda n: pl.BlockSpec((tm, n), lambda i: (i, 0))
    return pl.pallas_call(
        functools.partial(_out_ffn_kernel, tm=tm, tiles_per_seq=s // tm, final_norm=final_norm),
        grid=(t // tm,),
        in_specs=[row(D_MODEL), row(SB_WIDTH), row(SSM_INNER), row(MLA_WIDTH)]
                 + [_layer_spec(a, layer) for a in params]
                 + [pl.BlockSpec(fin.shape, lambda i: (0, 0), pipeline_mode=pl.Buffered(1))],
        out_specs=row(D_MODEL),
        out_shape=jax.ShapeDtypeStruct((t, D_MODEL), F32),
        scratch_shapes=[pltpu.VMEM((tm, D_MODEL), BF16), pltpu.VMEM((tm, D_MODEL), F32),
                        pltpu.VMEM((nj, SUBLANES, tf), F32), pltpu.VMEM((nj, SUBLANES, tf), F32),
                        ] + [pltpu.VMEM((SUBLANES + tm, tf), F32)] * 4 + [pltpu.VMEM((tm, tf), BF16)] * 2,
        compiler_params=pltpu.CompilerParams(
            dimension_semantics=("arbitrary",), vmem_limit_bytes=VMEM_LIMIT_FFN),
        name="out_ffn",
    )(h, ya, yb, yc, *params, fin)


def _zeros_like_cols(a, n):
    return jnp.zeros(a.shape[:-1] + (n,), a.dtype)


def _pack_in_proj(w):
    o_dt = C_CQ
    o_cq = o_dt + SSM_HEADS
    o_kr = o_cq + MLA_Q_RANK + MLA_KV_RANK
    kr = w[..., o_kr:o_kr + MLA_ROPE]
    z16, z32 = _zeros_like_cols(w, 16), _zeros_like_cols(w, 32)
    tail = jnp.concatenate([w[..., o_cq:o_kr], z32, kr[..., :HALF], z16, z32, kr[..., HALF:], z16,
                            w[..., o_dt:o_cq], _zeros_like_cols(w, LANES - SSM_HEADS)], axis=-1)
    return w.astype(BF16), tail.astype(BF16)


def _pack_uq(w):
    l, r, _ = w.shape
    w = w.reshape(l, r, MLA_HEADS, MLA_NOPE + MLA_ROPE)
    nope, rp = w[..., :MLA_NOPE], w[..., MLA_NOPE:]
    z16 = _zeros_like_cols(w, 16)
    out = jnp.concatenate([nope[..., :32], rp[..., :HALF], z16, nope[..., 32:], rp[..., HALF:], z16],
                          axis=-1)
    return out.reshape(l, r, MLA_HEADS * MLA_PAD).astype(BF16)


def _pack_ukv(w):
    l, r, _ = w.shape
    w = w.reshape(l, r, MLA_HEADS, MLA_NOPE + MLA_V)
    nope, wv = w[..., :MLA_NOPE], w[..., MLA_NOPE:]
    z32, z64 = _zeros_like_cols(w, 32), _zeros_like_cols(w, LANES - MLA_V)
    wk = jnp.concatenate([nope[..., :32], z32, nope[..., 32:], z32], axis=-1)
    odd = (jnp.arange(MLA_HEADS) % 2 == 1)[None, None, :, None]
    wv = jnp.where(odd, jnp.concatenate([z64, wv], axis=-1), jnp.concatenate([wv, z64], axis=-1))
    flat = lambda a: a.reshape(l, r, MLA_HEADS * LANES).astype(BF16)
    return flat(wk), flat(wv)


def _row(v, width=None):
    if width is not None and width > v.shape[-1]:
        v = jnp.concatenate([v, _zeros_like_cols(v, width - v.shape[-1])], axis=-1)
    return v[:, None, :]


TM_IN = 1024
TQ_SB = 256
TQ_MLA = 1024
TK_MLA = 512
SSD_BLOCK_CHUNKS = 4
TM_FFN = 1024
TF_FFN = 256
FFN_UNROLL = 2


def kernel(x, positions, mix_norm, w_in, sb_out_norm, ssm_conv_w, ssm_conv_b, ssm_dt_bias, ssm_a_log, ssm_d, ssm_out_norm, mla_q_norm, mla_w_uq, mla_kv_norm, mla_w_ukv, mla_out_norm, w_out, ffn_norm, ffn_w_up, ffn_conv_w, ffn_conv_b, ffn_w_down, final_norm):
    b, s, d = x.shape
    t = b * s
    depth = w_in.shape[0]
    h = x.reshape(t, d)

    inv_freq = 1.0 / (ROPE_THETA ** (jnp.arange(0, MLA_ROPE, 2, dtype=F32) / MLA_ROPE))
    z16, z32 = jnp.zeros((16,), F32), jnp.zeros((32,), F32)
    invf_lane = jnp.concatenate([z32, inv_freq, z16, z32, inv_freq, z16])[None, :]
    sign_lane = jnp.concatenate([z32, -jnp.ones((16,), F32), z16, z32, jnp.ones((16,), F32), z16])[None, :]
    rope_in = (positions.reshape(t, 1), invf_lane, sign_lane)

    w_main, w_tail = _pack_in_proj(w_in)
    w_uk, w_uv = _pack_ukv(mla_w_ukv)
    in_params = (_row(mix_norm), w_main, w_tail, _row(mla_q_norm), _pack_uq(mla_w_uq),
                 _row(mla_kv_norm), w_uk, w_uv)
    ssd_params = (ssm_conv_w, _row(ssm_conv_b), _row(ssm_dt_bias, LANES), _row(ssm_a_log, LANES),
                  _row(jnp.repeat(ssm_d, SSM_HEAD_DIM, axis=-1)), _row(ssm_out_norm))
    out_params = (_row(sb_out_norm), _row(mla_out_norm), w_out.astype(BF16), _row(ffn_norm),
                  ffn_w_up.astype(BF16), ffn_conv_w, _row(ffn_conv_b), ffn_w_down.astype(BF16))

    for i in range(depth):
        sb, z, xbc, dt, qm, km, vm, *tables = _mix_in(h, in_params, rope_in, i, TM_IN)
        rope_in = tuple(tables) or rope_in
        ya = _sb_attn(sb, b, s, TQ_SB)
        yb = _ssd(xbc, z, dt, ssd_params, i, b, s, SSD_BLOCK_CHUNKS)
        yc = _mla_attn(qm, km, vm, b, s, TQ_MLA, TK_MLA)
        h = _out_ffn(h, ya, yb, yc, out_params, final_norm[None, :], i, s, TM_FFN, TF_FFN,
                     final_norm=(i == depth - 1))
    return h.reshape(b, s, d)
```

```python
import functools

import jax
import jax.numpy as jnp
from jax import lax
from jax.experimental import pallas as pl
from jax.experimental.pallas import tpu as pltpu

F32 = jnp.float32
BF16 = jnp.bfloat16

D_MODEL = 1024
EPS = 1e-6
SB_HEADS = 4
SB_HEAD_DIM = 64
SB_WIDTH = SB_HEADS * SB_HEAD_DIM
SSM_HEADS = 8
SSM_HEAD_DIM = 64
SSM_INNER = SSM_HEADS * SSM_HEAD_DIM
SSM_GROUPS = 2
SSM_STATE = 64
SSM_CONV = 4
SSM_CHUNK = 128
SSM_CONV_DIM = SSM_INNER + 2 * SSM_GROUPS * SSM_STATE
MLA_HEADS = 4
MLA_NOPE = 64
MLA_ROPE = 32
MLA_V = 64
MLA_Q_RANK = 256
MLA_KV_RANK = 128
MLA_WIDTH = MLA_HEADS * MLA_V
ROPE_THETA = 10000.0
D_MIX = SB_WIDTH + SSM_INNER + MLA_WIDTH
D_FF = 2816
FFN_CONV = 3

LANES = 128
SUBLANES = 8
MLA_PAD = 128
HALF = MLA_ROPE // 2
VMEM_LIMIT = 48 * 1024 * 1024
VMEM_LIMIT_FFN = 62 * 1024 * 1024

C_SB = 0
C_Z = C_SB + 3 * SB_WIDTH
C_XBC = C_Z + SSM_INNER
C_CQ = C_XBC + SSM_CONV_DIM
C_CKV = C_CQ + MLA_Q_RANK
C_KPE = C_CKV + MLA_KV_RANK
C_DT = C_KPE + MLA_PAD
N_IN = C_DT + LANES

MLA_Q_SCALE = (MLA_NOPE + MLA_ROPE) ** -0.5 * 1.4426950408889634

SB_Q_SCALE = SB_HEAD_DIM ** -0.5 * 1.4426950408889634
SB_LOG2_FLOOR = -151.0


def _rms(x, g):
    return x * lax.rsqrt(jnp.mean(x * x, axis=-1, keepdims=True) + EPS) * g


def _silu(x):
    return x * (1.0 / (1.0 + jnp.exp(-x)))


def _softplus(x):
    return jnp.maximum(x, 0.0) + jnp.log1p(jnp.exp(-jnp.abs(x)))


def _split3(x):
    a = x.astype(BF16)
    r = x - a.astype(F32)
    b = r.astype(BF16)
    c = (r - b.astype(F32)).astype(BF16)
    return a, b, c


def _lane_tile(x, n):
    return jnp.concatenate([x] * n, axis=1)


def _dot(a, b):
    return jnp.dot(a, b, preferred_element_type=F32)


def _dot_nt(a, b):
    return lax.dot_general(a, b, (((1,), (1,)), ((), ())), preferred_element_type=F32)


def _layer_spec(a, layer):
    rest = a.shape[1:]
    return pl.BlockSpec((None,) + rest, lambda *_: (layer,) + (0,) * len(rest),
                        pipeline_mode=pl.Buffered(1))


def _mix_in_kernel(h_ref, g_ref, w_ref, wt_ref, qn_ref, wuq_ref, kvn_ref, wk_ref, wv_ref, *rest,
                   make_tables):
    if make_tables:
        (pos_ref, invf_ref, sign_ref, sb_ref, z_ref, xbc_ref, dt_ref, qm_ref, km_ref, vm_ref,
         cos_out, sin_out) = rest
        ang = pos_ref[...].astype(F32) * invf_ref[...]
        cos = jnp.cos(ang)
        sin = jnp.sin(ang) * sign_ref[...]
        cos_out[...] = cos
        sin_out[...] = sin
    else:
        cos_ref, sin_ref, sb_ref, z_ref, xbc_ref, dt_ref, qm_ref, km_ref, vm_ref = rest
        cos = cos_ref[...]
        sin = sin_ref[...]
    xn = _rms(h_ref[...], g_ref[...]).astype(BF16)

    def proj(lo, hi):
        if lo >= C_CQ:
            return _dot(xn, wt_ref[:, lo - C_CQ:hi - C_CQ])
        return _dot(xn, w_ref[:, lo:hi])

    def rope(x):
        return x * cos + pltpu.roll(x, LANES // 2, axis=1) * sin

    cq_raw = proj(C_CQ, C_CKV)
    ckv_raw = proj(C_CKV, C_KPE)
    kpe_raw = proj(C_KPE, C_DT)
    sb = proj(C_SB, C_Z)
    sb_ref[:, :SB_WIDTH] = (sb[:, :SB_WIDTH] * SB_Q_SCALE).astype(BF16)
    sb_ref[:, SB_WIDTH:] = sb[:, SB_WIDTH:].astype(BF16)

    cq = _rms(cq_raw, qn_ref[...]).astype(BF16)
    ckv = _rms(ckv_raw, kvn_ref[...]).astype(BF16)
    qf = _dot(cq, wuq_ref[...])
    kn = _dot(ckv, wk_ref[...])
    lane = lax.broadcasted_iota(jnp.int32, (1, MLA_HEADS * LANES), 1)
    ones = ((lane % LANES < MLA_V) == ((lane // LANES) % 2 == 1)).astype(F32)
    vm_ref[...] = (_dot(ckv, wv_ref[...]) + ones).astype(BF16)

    z_ref[...] = proj(C_Z, C_XBC)
    xbc_ref[...] = proj(C_XBC, C_CQ)
    dt_ref[...] = proj(C_DT, N_IN)

    kpe = rope(kpe_raw)
    for h in range(MLA_HEADS):
        sl = slice(h * MLA_PAD, (h + 1) * MLA_PAD)
        qm_ref[:, sl] = (rope(qf[:, sl]) * MLA_Q_SCALE).astype(BF16)
        km_ref[:, sl] = (kn[:, sl] + kpe).astype(BF16)


def _mix_in(h, params, rope_in, layer, tm):
    t = h.shape[0]
    make_tables = len(rope_in) == 3
    row = lambda n: pl.BlockSpec((tm, n), lambda i: (i, 0))
    lane_row = pl.BlockSpec((1, LANES), lambda i: (0, 0))
    out_widths = [(3 * SB_WIDTH, BF16), (SSM_INNER, F32), (SSM_CONV_DIM, F32), (LANES, F32),
                  (MLA_HEADS * MLA_PAD, BF16), (MLA_HEADS * MLA_PAD, BF16),
                  (MLA_HEADS * LANES, BF16)]
    if make_tables:
        rope_specs = [row(1), lane_row, lane_row]
        out_widths += [(LANES, F32), (LANES, F32)]
    else:
        rope_specs = [row(LANES), row(LANES)]
    param_specs = [_layer_spec(a, layer) for a in params]
    param_specs[1] = pl.BlockSpec((None, D_MODEL, C_CQ), lambda *_: (layer, 0, 0),
                                  pipeline_mode=pl.Buffered(1))
    return pl.pallas_call(
        functools.partial(_mix_in_kernel, make_tables=make_tables),
        grid=(t // tm,),
        in_specs=[row(D_MODEL)] + param_specs + rope_specs,
        out_specs=[row(n) for n, _ in out_widths],
        out_shape=[jax.ShapeDtypeStruct((t, n), d) for n, d in out_widths],
        compiler_params=pltpu.CompilerParams(dimension_semantics=("parallel",),
                                             vmem_limit_bytes=VMEM_LIMIT),
        name="mix_in",
    )(h, *params, *rope_in)


def _sb_kernel(q_ref, k_ref, v_ref, o_ref, c_ref, acc_ref, *, tq):
    tk = tq
    qi = pl.program_id(1)
    lane = lax.broadcasted_iota(jnp.int32, (tq, LANES), 1)
    c_ref[...] = jnp.zeros(c_ref.shape, F32)
    acc_ref[...] = jnp.zeros(acc_ref.shape, F32)

    heads = range(SB_HEADS)
    pair = [slice((h // 2) * LANES, (h // 2 + 1) * LANES) for h in heads]

    def walk(tiles):
        upper = (lax.broadcasted_iota(jnp.int32, (tk, tk), 0) >
                 lax.broadcasted_iota(jnp.int32, (tk, tk), 1)).astype(BF16)
        mask = (lax.broadcasted_iota(jnp.int32, (tq, tk), 1) <
                lax.broadcasted_iota(jnp.int32, (tq, tk), 0))
        ks = [pl.multiple_of(j * tk, tk) for j, _ in tiles]
        qh = []
        for h in heads:
            own = (lane < SB_HEAD_DIM) if h % 2 == 0 else (lane >= SB_HEAD_DIM)
            qh.append(jnp.where(own, q_ref[:, pair[h]], jnp.zeros((), BF16)))
        z = [[_dot_nt(qh[h], k_ref[pl.ds(k0, tk), pair[h]]) for h in heads]
             for k0 in ks]
        log_keep = [[-(jnp.maximum(x, 0.0) + jnp.log2(1.0 + jnp.exp2(-jnp.abs(x)))) for x in zt]
                    for zt in z]
        log_keep = [[jnp.where(mask, x, 0.0) for x in lt] if diag else lt
                    for lt, (_, diag) in zip(log_keep, tiles)]
        tail = []
        for lt in log_keep:
            hi = [x.astype(BF16) for x in lt]
            lo = [(x - y.astype(F32)).astype(BF16) for x, y in zip(lt, hi)]
            tail.append([_dot(hi[h], upper) + _dot(lo[h], upper) for h in heads])
        pv = [None] * SB_HEADS
        c = [c_ref[h] for h in heads]
        for t, (_, diag) in enumerate(tiles):
            for h in heads:
                later = tail[t][h] + _lane_tile(c[h], tk // LANES)
                w = jnp.exp2(z[t][h] + log_keep[t][h] + later)
                if diag:
                    w = jnp.where(mask, w, 0.0)
                p = _dot(w.astype(BF16), v_ref[pl.ds(ks[t], tk), pair[h]])
                pv[h] = p if pv[h] is None else pv[h] + p
                c[h] = c[h] + jnp.sum(log_keep[t][h], axis=1, keepdims=True)
        for h in heads:
            c_ref[h] = c[h]
            acc_ref[h] += pv[h]
        return jnp.max(functools.reduce(jnp.maximum, c))

    def cond(carry):
        j, c_top = carry
        return jnp.logical_and(j >= 0, c_top > SB_LOG2_FLOOR)

    def body(carry):
        j, _ = carry
        return j - 1, walk([(j, False)])

    start = lax.cond(qi > 0,
                     lambda: (qi - 2, walk([(qi, True), (qi - 1, False)])),
                     lambda: (qi - 1, walk([(qi, True)])))
    lax.while_loop(cond, body, start)
    for hp in range(SB_HEADS // 2):
        o_ref[:, hp * LANES:(hp + 1) * LANES] = jnp.where(lane < SB_HEAD_DIM, acc_ref[2 * hp],
                                                          acc_ref[2 * hp + 1])


def _sb_attn(sb, b, s, tq):
    nq = s // tq
    return pl.pallas_call(
        functools.partial(_sb_kernel, tq=tq),
        grid=(b, nq),
        in_specs=[pl.BlockSpec((tq, SB_WIDTH), lambda bi, qi: (bi * nq + qi, 0)),
                  pl.BlockSpec((s, SB_WIDTH), lambda bi, qi: (bi, 1)),
                  pl.BlockSpec((s, SB_WIDTH), lambda bi, qi: (bi, 2))],
        out_specs=pl.BlockSpec((tq, SB_WIDTH), lambda bi, qi: (bi * nq + qi, 0)),
        out_shape=jax.ShapeDtypeStruct((b * s, SB_WIDTH), F32),
        scratch_shapes=[pltpu.VMEM((SB_HEADS, tq, LANES), F32),
                        pltpu.VMEM((SB_HEADS, tq, LANES), F32)],
        compiler_params=pltpu.CompilerParams(dimension_semantics=("parallel", "parallel"),
                                             vmem_limit_bytes=VMEM_LIMIT),
        name="sb_attn",
    )(sb, sb, sb)


def _ssd_kernel(xbc_ref, z_ref, dt_ref, cw_ref, cb_ref, dtb_ref, alog_ref, dl_ref, gn_ref,
                y_ref, tail_ref, xs_ref, state_ref, *, nch):
    l = SSM_CHUNK
    rows = nch * l
    hd = SSM_HEAD_DIM
    pairs = SSM_HEADS // 2

    @pl.when(pl.program_id(1) == 0)
    def _():
        tail_ref[...] = jnp.zeros_like(tail_ref)
        state_ref[...] = jnp.zeros_like(state_ref)

    xs_ref[0:SUBLANES, :] = tail_ref[...]
    xs_ref[SUBLANES:SUBLANES + rows, :] = xbc_ref[...]
    tail_ref[...] = xbc_ref[rows - SUBLANES:rows, :]
    conv = cb_ref[...]
    for k in range(SSM_CONV):
        conv = conv + cw_ref[k:k + 1, :] * xs_ref[pl.ds(SUBLANES - (SSM_CONV - 1) + k, rows), :]
    xa = _silu(conv)
    dt_all = _softplus(dt_ref[...] + dtb_ref[...])
    da_all = dt_all * (-jnp.exp(alog_ref[...]))

    tril = (lax.broadcasted_iota(jnp.int32, (l, l), 0) >=
            lax.broadcasted_iota(jnp.int32, (l, l), 1))
    trilb = tril.astype(BF16)
    lo_half = lax.broadcasted_iota(jnp.int32, (l, LANES), 1) < hd
    lo_half_s = lax.broadcasted_iota(jnp.int32, (SSM_STATE, LANES), 1) < hd
    gn = SSM_GROUPS * SSM_STATE

    for c in range(nch):
        rs = slice(c * l, (c + 1) * l)
        x = xa[rs, :SSM_INNER]
        xb = x.astype(BF16)
        bm = xa[rs, SSM_INNER:SSM_INNER + gn]
        cm = xa[rs, SSM_INNER + gn:]
        d1, d2, d3 = _split3(da_all[rs])
        cs = _dot(trilb, d1) + _dot(trilb, d2) + _dot(trilb, d3)
        cs8 = cs.T[0:SSM_HEADS]
        dt8 = dt_all[rs].T[0:SSM_HEADS]
        bm_t = bm.T
        end8 = cs8[:, l - 1:l]
        w8 = jnp.exp(end8 - cs8) * dt8
        dec8 = jnp.broadcast_to(jnp.exp(end8), (SSM_HEADS, LANES))
        cg = [cm[:, g * SSM_STATE:(g + 1) * SSM_STATE].astype(BF16) for g in range(SSM_GROUPS)]
        bg_t = [bm_t[g * SSM_STATE:(g + 1) * SSM_STATE] for g in range(SSM_GROUPS)]
        cb = [_dot(cg[g], bg_t[g].astype(BF16)) for g in range(SSM_GROUPS)]

        ys = []
        for p in range(pairs):
            g = (2 * p) // (SSM_HEADS // SSM_GROUPS)
            ps = slice(p * LANES, (p + 1) * LANES)
            xp = xb[:, ps]
            y_diag, contrib, col = [], [], []
            for h in (2 * p, 2 * p + 1):
                cs_col = jnp.broadcast_to(cs[:, h:h + 1], (l, l))
                decay = jnp.exp(cs_col - cs8[h:h + 1, :])
                m = jnp.where(tril, cb[g] * decay * dt8[h:h + 1, :], 0.0)
                y_diag.append(_dot(m.astype(BF16), xp))
                contrib.append(_dot((bg_t[g] * w8[h:h + 1, :]).astype(BF16), xp))
                col.append(cs_col)
            st = state_ref[p]
            y_off = _dot(cg[g], st.astype(BF16)) * jnp.exp(jnp.where(lo_half, col[0], col[1]))
            st_decay = jnp.where(lo_half_s, dec8[2 * p:2 * p + 1, :], dec8[2 * p + 1:2 * p + 2, :])
            state_ref[p] = st * st_decay + jnp.where(lo_half_s, contrib[0], contrib[1])
            ys.append(jnp.where(lo_half, y_diag[0], y_diag[1]) + y_off + x[:, ps] * dl_ref[:, ps])
        y = jnp.concatenate(ys, axis=1) * _silu(z_ref[rs, :])
        y_ref[rs, :] = _rms(y, gn_ref[...]).astype(BF16)


def _ssd(xbc, z, dt, params, layer, b, s, nch):
    rows = nch * SSM_CHUNK
    nb = s // rows
    row = lambda n: pl.BlockSpec((rows, n), lambda bi, ci: (bi * nb + ci, 0))
    return pl.pallas_call(
        functools.partial(_ssd_kernel, nch=nch),
        grid=(b, nb),
        in_specs=[row(SSM_CONV_DIM), row(SSM_INNER), row(LANES)]
                 + [_layer_spec(a, layer) for a in params],
        out_specs=row(SSM_INNER),
        out_shape=jax.ShapeDtypeStruct((b * s, SSM_INNER), BF16),
        scratch_shapes=[pltpu.VMEM((SUBLANES, SSM_CONV_DIM), F32),
                        pltpu.VMEM((SUBLANES + rows, SSM_CONV_DIM), F32),
                        pltpu.VMEM((SSM_HEADS // 2, SSM_STATE, LANES), F32)],
        compiler_params=pltpu.CompilerParams(dimension_semantics=("parallel", "arbitrary"),
                                             vmem_limit_bytes=VMEM_LIMIT),
        name="ssd",
    )(xbc, z, dt, *params)


def _mla_kernel(q_ref, k_ref, v_ref, o_ref, m_ref, acc_ref, *, tq, tk):
    qi = pl.program_id(1)
    diag_tiles = tq // tk
    m_ref[...] = jnp.full(m_ref.shape, -1e30, F32)
    acc_ref[...] = jnp.zeros(acc_ref.shape, F32)

    heads = range(MLA_HEADS)

    def step(ks, nk, diag):
        r0 = 0 if diag is None else diag * tk
        rs = slice(r0, tq)
        nr = tq - r0
        sc = []
        for h in heads:
            hs = slice(h * MLA_PAD, (h + 1) * MLA_PAD)
            sc.append(_dot_nt(q_ref[rs, hs], k_ref[pl.ds(ks, nk), hs]))
        if diag is not None:
            keep = (lax.broadcasted_iota(jnp.int32, (nr, nk), 1) <=
                    lax.broadcasted_iota(jnp.int32, (nr, nk), 0))
            sc = [jnp.where(keep, x, -jnp.inf) for x in sc]
        alpha, pv = [], []
        for h in heads:
            m_old = m_ref[h, rs, :]
            m_new = jnp.maximum(m_old, jnp.max(sc[h], axis=1, keepdims=True))
            p = jnp.exp2(sc[h] - _lane_tile(m_new, nk // LANES))
            m_ref[h, rs, :] = m_new
            alpha.append(jnp.exp2(m_old - m_new))
            pv.append(_dot(p.astype(BF16), v_ref[pl.ds(ks, nk), h * LANES:(h + 1) * LANES]))
        for h in heads:
            acc_ref[h, rs, :] = alpha[h] * acc_ref[h, rs, :] + pv[h]

    def body(j, carry):
        step(pl.multiple_of(j * tq, tq), tq, None)
        return carry

    lax.fori_loop(0, qi, body, 0)
    for u in range(diag_tiles):
        step(pl.multiple_of(qi * tq + u * tk, tk), tk, u)
    lane = lax.broadcasted_iota(jnp.int32, (tq, LANES), 1)
    for hp in range(MLA_HEADS // 2):
        even = acc_ref[2 * hp]
        odd = acc_ref[2 * hp + 1]
        even = even / pltpu.roll(even, LANES // 2, axis=1)
        odd = odd / pltpu.roll(odd, LANES // 2, axis=1)
        o_ref[:, hp * LANES:(hp + 1) * LANES] = jnp.where(lane < MLA_V, even, odd)


def _mla_attn(qm, km, vm, b, s, tq, tk):
    nq = s // tq
    return pl.pallas_call(
        functools.partial(_mla_kernel, tq=tq, tk=tk),
        grid=(b, nq),
        in_specs=[pl.BlockSpec((tq, MLA_HEADS * MLA_PAD), lambda bi, qi: (bi * nq + qi, 0)),
                  pl.BlockSpec((s, MLA_HEADS * MLA_PAD), lambda bi, qi: (bi, 0),
                               pipeline_mode=pl.Buffered(1)),
                  pl.BlockSpec((s, MLA_HEADS * LANES), lambda bi, qi: (bi, 0),
                               pipeline_mode=pl.Buffered(1))],
        out_specs=pl.BlockSpec((tq, MLA_WIDTH), lambda bi, qi: (bi * nq + qi, 0)),
        out_shape=jax.ShapeDtypeStruct((b * s, MLA_WIDTH), F32),
        scratch_shapes=[pltpu.VMEM((MLA_HEADS, tq, LANES), F32),
                        pltpu.VMEM((MLA_HEADS, tq, LANES), F32)],
        compiler_params=pltpu.CompilerParams(dimension_semantics=("parallel", "parallel"),
                                             vmem_limit_bytes=VMEM_LIMIT_FFN),
        name="mla_attn",
    )(qm, km, vm)


def _out_ffn_kernel(h_ref, ya_ref, yb_ref, yc_ref, ga_ref, gc_ref, wo_ref, g_ref, wup_ref, cw_ref,
                    cb_ref, wd_ref, fin_ref,
                    o_ref, hn_ref, acc_ref, carry_g, carry_v, sg0, sg1, sv0, sv1, act0, act1, *, tm,
                    tiles_per_seq, final_norm):
    c1 = SB_WIDTH
    c2 = SB_WIDTH + SSM_INNER
    h1 = h_ref[...] + _dot(yb_ref[...], wo_ref[c1:c2, :])
    ya = _rms(ya_ref[...], ga_ref[...]).astype(BF16)
    yc = _rms(yc_ref[...], gc_ref[...]).astype(BF16)
    h1 = h1 + _dot(ya, wo_ref[:c1, :]) + _dot(yc, wo_ref[c2:, :])
    hn_ref[...] = _rms(h1, g_ref[...]).astype(BF16)
    acc_ref[...] = h1

    first = (pl.program_id(0) % tiles_per_seq) == 0
    pad = SUBLANES
    tf = sg0.shape[1]
    nj = D_FF // tf

    def cols(half, j):
        return pl.ds(pl.multiple_of(half * D_FF + j * tf, tf), tf)

    sg, sv, act = (sg0, sg1), (sv0, sv1), (act0, act1)

    def up_proj(j, slot):
        for half, carry_ref, s_ref in ((0, carry_g, sg[slot]), (1, carry_v, sv[slot])):
            up = _dot(hn_ref[...], wup_ref[:, cols(half, j)])
            s_ref[0:pad, :] = jnp.where(first, 0.0, carry_ref[j])
            s_ref[pad:pad + tm, :] = up
            carry_ref[j] = up[tm - pad:tm, :]

    def conv_act(j, slot):
        def conv(half, s_ref):
            cw = cw_ref[:, cols(half, j)]
            return (cb_ref[:, cols(half, j)] + cw[0:1, :] * s_ref[pl.ds(pad - 2, tm), :]
                    + cw[1:2, :] * s_ref[pl.ds(pad - 1, tm), :]
                    + cw[2:3, :] * s_ref[pl.ds(pad, tm), :])

        act[slot][...] = (_silu(conv(0, sg[slot])) * conv(1, sv[slot])).astype(BF16)

    def down_proj(j, slot):
        acc_ref[...] += _dot(act[slot][...], wd_ref[pl.ds(pl.multiple_of(j * tf, tf), tf), :])

    def steady(j, slot):
        up_proj(j + 1, 1 - slot)
        conv_act(j, slot)
        down_proj(j, slot)

    up_proj(0, 0)
    trips = (nj - 1) // FFN_UNROLL

    def group(p, carry):
        for u in range(FFN_UNROLL):
            steady(FFN_UNROLL * p + u, u % 2)
        return carry

    lax.fori_loop(0, trips, group, 0)
    for j in range(trips * FFN_UNROLL, nj - 1):
        steady(j, j % 2)
    conv_act(nj - 1, (nj - 1) % 2)
    down_proj(nj - 1, (nj - 1) % 2)
    out = acc_ref[...]
    if final_norm:
        out = _rms(out, fin_ref[...])
    o_ref[...] = out


def _out_ffn(h, ya, yb, yc, params, fin, layer, s, tm, tf, final_norm):
    t = h.shape[0]
    nj = D_FF // tf
    row = lambda n: pl.BlockSpec((tm, n), lambda i: (i, 0))
    return pl.pallas_call(
        functools.partial(_out_ffn_kernel, tm=tm, tiles_per_seq=s // tm, final_norm=final_norm),
        grid=(t // tm,),
        in_specs=[row(D_MODEL), row(SB_WIDTH), row(SSM_INNER), row(MLA_WIDTH)]
                 + [_layer_spec(a, layer) for a in params]
                 + [pl.BlockSpec(fin.shape, lambda i: (0, 0), pipeline_mode=pl.Buffered(1))],
        out_specs=row(D_MODEL),
        out_shape=jax.ShapeDtypeStruct((t, D_MODEL), F32),
        scratch_shapes=[pltpu.VMEM((tm, D_MODEL), BF16), pltpu.VMEM((tm, D_MODEL), F32),
                        pltpu.VMEM((nj, SUBLANES, tf), F32), pltpu.VMEM((nj, SUBLANES, tf), F32),
                        ] + [pltpu.VMEM((SUBLANES + tm, tf), F32)] * 4 + [pltpu.VMEM((tm, tf), BF16)] * 2,
        compiler_params=pltpu.CompilerParams(
            dimension_semantics=("arbitrary",), vmem_limit_bytes=VMEM_LIMIT_FFN),
        name="out_ffn",
    )(h, ya, yb, yc, *params, fin)


def _zeros_like_cols(a, n):
    return jnp.zeros(a.shape[:-1] + (n,), a.dtype)


def _pack_in_proj(w):
    o_dt = C_CQ
    o_cq = o_dt + SSM_HEADS
    o_kr = o_cq + MLA_Q_RANK + MLA_KV_RANK
    kr = w[..., o_kr:o_kr + MLA_ROPE]
    z16, z32 = _zeros_like_cols(w, 16), _zeros_like_cols(w, 32)
    tail = jnp.concatenate([w[..., o_cq:o_kr], z32, kr[..., :HALF], z16, z32, kr[..., HALF:], z16,
                            w[..., o_dt:o_cq], _zeros_like_cols(w, LANES - SSM_HEADS)], axis=-1)
    return w.astype(BF16), tail.astype(BF16)


def _pack_uq(w):
    l, r, _ = w.shape
    w = w.reshape(l, r, MLA_HEADS, MLA_NOPE + MLA_ROPE)
    nope, rp = w[..., :MLA_NOPE], w[..., MLA_NOPE:]
    z16 = _zeros_like_cols(w, 16)
    out = jnp.concatenate([nope[..., :32], rp[..., :HALF], z16, nope[..., 32:], rp[..., HALF:], z16],
                          axis=-1)
    return out.reshape(l, r, MLA_HEADS * MLA_PAD).astype(BF16)


def _pack_ukv(w):
    l, r, _ = w.shape
    w = w.reshape(l, r, MLA_HEADS, MLA_NOPE + MLA_V)
    nope, wv = w[..., :MLA_NOPE], w[..., MLA_NOPE:]
    z32, z64 = _zeros_like_cols(w, 32), _zeros_like_cols(w, LANES - MLA_V)
    wk = jnp.concatenate([nope[..., :32], z32, nope[..., 32:], z32], axis=-1)
    odd = (jnp.arange(MLA_HEADS) % 2 == 1)[None, None, :, None]
    wv = jnp.where(odd, jnp.concatenate([z64, wv], axis=-1), jnp.concatenate([wv, z64], axis=-1))
    flat = lambda a: a.reshape(l, r, MLA_HEADS * LANES).astype(BF16)
    return flat(wk), flat(wv)


def _row(v, width=None):
    if width is not None and width > v.shape[-1]:
        v = jnp.concatenate([v, _zeros_like_cols(v, width - v.shape[-1])], axis=-1)
    return v[:, None, :]


TM_IN = 1024
TQ_SB = 256
TQ_MLA = 1024
TK_MLA = 512
SSD_BLOCK_CHUNKS = 4
TM_FFN = 1024
TF_FFN = 256
FFN_UNROLL = 2


def kernel(x, positions, mix_norm, w_in, sb_out_norm, ssm_conv_w, ssm_conv_b, ssm_dt_bias, ssm_a_log, ssm_d, ssm_out_norm, mla_q_norm, mla_w_uq, mla_kv_norm, mla_w_ukv, mla_out_norm, w_out, ffn_norm, ffn_w_up, ffn_conv_w, ffn_conv_b, ffn_w_down, final_norm):
    b, s, d = x.shape
    t = b * s
    depth = w_in.shape[0]
    h = x.reshape(t, d)

    inv_freq = 1.0 / (ROPE_THETA ** (jnp.arange(0, MLA_ROPE, 2, dtype=F32) / MLA_ROPE))
    z16, z32 = jnp.zeros((16,), F32), jnp.zeros((32,), F32)
    invf_lane = jnp.concatenate([z32, inv_freq, z16, z32, inv_freq, z16])[None, :]
    sign_lane = jnp.concatenate([z32, -jnp.ones((16,), F32), z16, z32, jnp.ones((16,), F32), z16])[None, :]
    rope_in = (positions.reshape(t, 1), invf_lane, sign_lane)

    w_main, w_tail = _pack_in_proj(w_in)
    w_uk, w_uv = _pack_ukv(mla_w_ukv)
    in_params = (_row(mix_norm), w_main, w_tail, _row(mla_q_norm), _pack_uq(mla_w_uq),
                 _row(mla_kv_norm), w_uk, w_uv)
    ssd_params = (ssm_conv_w, _row(ssm_conv_b), _row(ssm_dt_bias, LANES), _row(ssm_a_log, LANES),
                  _row(jnp.repeat(ssm_d, SSM_HEAD_DIM, axis=-1)), _row(ssm_out_norm))
    out_params = (_row(sb_out_norm), _row(mla_out_norm), w_out.astype(BF16), _row(ffn_norm),
                  ffn_w_up.astype(BF16), ffn_conv_w, _row(ffn_conv_b), ffn_w_down.astype(BF16))

    for i in range(depth):
        sb, z, xbc, dt, qm, km, vm, *tables = _mix_in(h, in_params, rope_in, i, TM_IN)
        rope_in = tuple(tables) or rope_in
        ya = _sb_attn(sb, b, s, TQ_SB)
        yb = _ssd(xbc, z, dt, ssd_params, i, b, s, SSD_BLOCK_CHUNKS)
        yc = _mla_attn(qm, km, vm, b, s, TQ_MLA, TK_MLA)
        h = _out_ffn(h, ya, yb, yc, out_params, final_norm[None, :], i, s, TM_FFN, TF_FFN,
                     final_norm=(i == depth - 1))
    return h.reshape(b, s, d)
```
